```python
import jax, jax.numpy as jnp
from jax import lax
import numpy as np

D_MODEL = 1024
BATCH = 4
SEQ = 4096
DEPTH = 2

D_MIX = D_MODEL
HEAD_DIM = 64
W_CONV = D_MIX // 4
W_SGU = D_MIX // 4
W_LRU = D_MIX // 2
H_CONV = W_CONV // HEAD_DIM
H_SGU = W_SGU // HEAD_DIM
H_LRU = W_LRU // HEAD_DIM
D_IN = 3 * W_CONV + 2 * W_SGU + 2 * W_LRU
CONV_A_WIDTH = 3
CONV_C_WIDTH = 4
CHUNK = 128
RGLRU_C = 8.0
N_EXPERTS = 32
TOP_K = 4
D_FF_EXPERT = D_MODEL
SWIGLU_LIMIT = 7.0
SWIGLU_ALPHA = 1.702
MOE_BLOCK = 128
N_MOD = 6
EPS = 1e-6

kernel_name = 'hybrid_conv_sgu_rglru_moe_adaln'


def rms_normalize(x):
    xf = x.astype(jnp.float32)
    return xf * lax.rsqrt(jnp.mean(xf * xf, axis=-1, keepdims=True) + EPS)


def rms_norm(x, g):
    return (rms_normalize(x) * g.astype(jnp.float32)).astype(x.dtype)


def layer_norm(x, g, b):
    xf = x.astype(jnp.float32)
    xc = xf - jnp.mean(xf, axis=-1, keepdims=True)
    y = xc * lax.rsqrt(jnp.mean(xc * xc, axis=-1, keepdims=True) + EPS)
    return (y * g.astype(jnp.float32) + b.astype(jnp.float32)).astype(x.dtype)


def causal_depthwise_conv(z, w):
    k_width = w.shape[0]
    seq = z.shape[1]
    zp = jnp.pad(z, ((0, 0), (k_width - 1, 0), (0, 0)))
    y = zp[:, 0:seq] * w[0]
    for k in range(1, k_width):
        y = y + zp[:, k:k + seq] * w[k]
    return y


def short_conv_mixer(b_gate, c_gate, v, conv_w):
    return b_gate * causal_depthwise_conv(c_gate * v, conv_w)


def chunked_spatial_gating(z, sgu_w, sgu_b, ln_g, ln_b):
    bsz, seq, _ = z.shape
    z = jax.nn.gelu(z)
    u, v = jnp.split(z, 2, axis=-1)
    v = layer_norm(v, ln_g, ln_b).reshape(bsz, seq // CHUNK, CHUNK, H_SGU, HEAD_DIM)
    causal = jnp.tril(jnp.ones((CHUNK, CHUNK), dtype=bool))
    w = jnp.where(causal[None], sgu_w, jnp.zeros((), sgu_w.dtype))
    mixed = jnp.einsum('hts,bnshd->bnthd', w, v) + jnp.transpose(sgu_b)[:, :, None]
    return u * mixed.reshape(bsz, seq, W_SGU)


def rg_lru(x, gate_w, gate_b, lam):
    bsz, seq, _ = x.shape
    xh = x.reshape(bsz, seq, H_LRU, HEAD_DIM)
    gates = jnp.einsum('bshi,ghij->gbshj', xh, gate_w) + gate_b[:, None, None]
    gates = jax.nn.sigmoid(gates.astype(jnp.float32)).reshape(2, bsz, seq, W_LRU)
    r, i = gates[0], gates[1]
    log_a = -RGLRU_C * r * jax.nn.softplus(-lam.astype(jnp.float32))
    a = jnp.exp(log_a)
    b = x.astype(jnp.float32) * i * jnp.sqrt(-jnp.expm1(2.0 * log_a))

    def combine(left, right):
        a_l, b_l = left
        a_r, b_r = right
        return a_l * a_r, a_r * b_l + b_r

    _, h = lax.associative_scan(combine, (a, b), axis=1)
    return h.astype(x.dtype)


def recurrent_branch(gate_in, x_in, conv_w, conv_b, gate_w, gate_b, lam):
    xr = causal_depthwise_conv(x_in, conv_w) + conv_b
    return jax.nn.gelu(gate_in) * rg_lru(xr, gate_w, gate_b, lam)


def hybrid_mixer(h, w_in, conv_a_w, sgu_w, sgu_b, sgu_ln_g, sgu_ln_b, conv_c_w, conv_c_b,
                 rglru_gate_w, rglru_gate_b, rglru_lambda, mix_norm_g, w_out):
    z = jnp.einsum('bsd,dp->bsp', h, w_in)
    cuts = [W_CONV, 2 * W_CONV, 3 * W_CONV, 3 * W_CONV + 2 * W_SGU, 3 * W_CONV + 2 * W_SGU + W_LRU]
    z_b, z_c, z_v, z_s, z_g, z_x = jnp.split(z, cuts, axis=-1)
    y_conv = short_conv_mixer(z_b, z_c, z_v, conv_a_w)
    y_sgu = chunked_spatial_gating(z_s, sgu_w, sgu_b, sgu_ln_g, sgu_ln_b)
    y_lru = recurrent_branch(z_g, z_x, conv_c_w, conv_c_b, rglru_gate_w, rglru_gate_b, rglru_lambda)
    y = jnp.concatenate([rms_normalize(y_conv), rms_normalize(y_sgu), rms_normalize(y_lru)], axis=-1)
    y = (y * mix_norm_g.astype(jnp.float32)).astype(h.dtype)
    return jnp.einsum('bsm,md->bsd', y, w_out)


def clamped_swiglu_expert(xb, w1, b1, w2, b2):
    hgu = xb @ w1 + b1
    g, u = jnp.split(hgu, 2, axis=-1)
    g = jnp.minimum(g, SWIGLU_LIMIT)
    u = jnp.clip(u, -SWIGLU_LIMIT, SWIGLU_LIMIT)
    return ((u + 1.0) * (g * jax.nn.sigmoid(SWIGLU_ALPHA * g))) @ w2 + b2


def moe_ffn(h, router_w, router_b, w1, b1, w2, b2):
    bsz, seq, d = h.shape
    n_tok = bsz * seq
    xt = h.reshape(n_tok, d)
    logits = xt.astype(jnp.float32) @ router_w.astype(jnp.float32) + router_b.astype(jnp.float32)
    top_logits, top_idx = lax.top_k(logits, TOP_K)
    top_w = jax.nn.softmax(top_logits, axis=-1)
    n_assign = n_tok * TOP_K
    flat_e = top_idx.reshape(n_assign)
    order = jnp.argsort(flat_e)
    sorted_e = flat_e[order]
    sorted_tok = (order // TOP_K).astype(jnp.int32)
    sorted_w = top_w.reshape(n_assign)[order]
    counts = jnp.bincount(flat_e, length=N_EXPERTS)
    padded = (counts + MOE_BLOCK - 1) // MOE_BLOCK * MOE_BLOCK
    pad_end = jnp.cumsum(padded)
    pad_start = pad_end - padded
    grp_start = jnp.cumsum(counts) - counts
    rank = jnp.arange(n_assign) - grp_start[sorted_e]
    slot = pad_start[sorted_e] + rank
    n_blocks = (n_assign + MOE_BLOCK - 1) // MOE_BLOCK + N_EXPERTS
    n_slots = n_blocks * MOE_BLOCK
    slot_tok = jnp.full((n_slots,), n_tok, jnp.int32).at[slot].set(sorted_tok)
    slot_w = jnp.zeros((n_slots,), jnp.float32).at[slot].set(sorted_w)
    x_pad = jnp.concatenate([xt, jnp.zeros((1, d), xt.dtype)], axis=0)
    x_blocks = x_pad[slot_tok].reshape(n_blocks, MOE_BLOCK, d)
    block_e = jnp.minimum(
        jnp.searchsorted(pad_end, jnp.arange(n_blocks) * MOE_BLOCK, side='right'), N_EXPERTS - 1)

    def run_block(args):
        xb, e = args
        return clamped_swiglu_expert(xb, w1[e], b1[e], w2[e], b2[e])

    y = lax.map(run_block, (x_blocks, block_e)).reshape(n_slots, d)
    y = y * slot_w[:, None].astype(y.dtype)
    out = jnp.zeros((n_tok + 1, d), y.dtype).at[slot_tok].add(y)[:n_tok]
    return out.reshape(bsz, seq, d)


def decoder_layer(x, c, norm1_g, norm2_g, w_ada, b_ada, w_in, conv_a_w, sgu_w, sgu_b, sgu_ln_g,
                  sgu_ln_b, conv_c_w, conv_c_b, rglru_gate_w, rglru_gate_b, rglru_lambda,
                  mix_norm_g, w_out, router_w, router_b, expert_w1, expert_b1, expert_w2, expert_b2):
    mod = (c @ w_ada + b_ada)[:, None, :]
    shift1, scale1, gate1, shift2, scale2, gate2 = jnp.split(mod, N_MOD, axis=-1)
    h = rms_norm(x, norm1_g) * (1.0 + scale1) + shift1
    x = x + gate1 * hybrid_mixer(h, w_in, conv_a_w, sgu_w, sgu_b, sgu_ln_g, sgu_ln_b, conv_c_w,
                                 conv_c_b, rglru_gate_w, rglru_gate_b, rglru_lambda, mix_norm_g, w_out)
    h = rms_norm(x, norm2_g) * (1.0 + scale2) + shift2
    x = x + gate2 * moe_ffn(h, router_w, router_b, expert_w1, expert_b1, expert_w2, expert_b2)
    return x


def setup_inputs(seed: int = 0) -> dict:
    key = jax.random.key(seed)
    ks = jax.random.split(key, 32)

    def nrm(k, shape, scale):
        return scale * jax.random.normal(k, shape, jnp.float32)

    u = jax.random.uniform(ks[16], (DEPTH, W_LRU), jnp.float32, minval=0.9, maxval=0.999)
    a0 = u ** (1.0 / RGLRU_C)
    rglru_lambda = jnp.log(a0) - jnp.log1p(-a0)
    return {
        'x': nrm(ks[0], (BATCH, SEQ, D_MODEL), 1.0),
        'c': nrm(ks[1], (BATCH, D_MODEL), 1.0),
        'norm1_g': 1.0 + nrm(ks[2], (DEPTH, D_MODEL), 0.02),
        'norm2_g': 1.0 + nrm(ks[3], (DEPTH, D_MODEL), 0.02),
        'w_ada': nrm(ks[4], (DEPTH, D_MODEL, N_MOD * D_MODEL), 0.2 * D_MODEL ** -0.5),
        'b_ada': nrm(ks[5], (DEPTH, N_MOD * D_MODEL), 0.02),
        'w_in': nrm(ks[6], (DEPTH, D_MODEL, D_IN), D_MODEL ** -0.5),
        'conv_a_w': nrm(ks[7], (DEPTH, CONV_A_WIDTH, W_CONV), CONV_A_WIDTH ** -0.5),
        'sgu_w': nrm(ks[8], (DEPTH, H_SGU, CHUNK, CHUNK), CHUNK ** -0.5),
        'sgu_b': 1.0 + nrm(ks[9], (DEPTH, H_SGU, CHUNK), 0.1),
        'sgu_ln_g': 1.0 + nrm(ks[10], (DEPTH, W_SGU), 0.02),
        'sgu_ln_b': nrm(ks[11], (DEPTH, W_SGU), 0.02),
        'conv_c_w': nrm(ks[12], (DEPTH, CONV_C_WIDTH, W_LRU), CONV_C_WIDTH ** -0.5),
        'conv_c_b': nrm(ks[13], (DEPTH, W_LRU), 0.02),
        'rglru_gate_w': nrm(ks[14], (DEPTH, 2, H_LRU, HEAD_DIM, HEAD_DIM), HEAD_DIM ** -0.5),
        'rglru_gate_b': nrm(ks[15], (DEPTH, 2, H_LRU, HEAD_DIM), 0.02),
        'rglru_lambda': rglru_lambda,
        'mix_norm_g': 1.0 + nrm(ks[17], (DEPTH, D_MIX), 0.02),
        'w_out': nrm(ks[18], (DEPTH, D_MIX, D_MODEL), D_MIX ** -0.5),
        'router_w': nrm(ks[19], (DEPTH, D_MODEL, N_EXPERTS), D_MODEL ** -0.5),
        'router_b': nrm(ks[20], (DEPTH, N_EXPERTS), 0.01),
        'expert_w1': nrm(ks[21], (DEPTH, N_EXPERTS, D_MODEL, 2 * D_FF_EXPERT), D_MODEL ** -0.5),
        'expert_b1': nrm(ks[22], (DEPTH, N_EXPERTS, 2 * D_FF_EXPERT), 0.01),
        'expert_w2': nrm(ks[23], (DEPTH, N_EXPERTS, D_FF_EXPERT, D_MODEL), D_FF_EXPERT ** -0.5),
        'expert_b2': nrm(ks[24], (DEPTH, N_EXPERTS, D_MODEL), 0.01),
        'final_norm_g': 1.0 + nrm(ks[25], (D_MODEL,), 0.02),
    }


def reference(x, c, norm1_g, norm2_g, w_ada, b_ada, w_in, conv_a_w, sgu_w, sgu_b, sgu_ln_g,
              sgu_ln_b, conv_c_w, conv_c_b, rglru_gate_w, rglru_gate_b, rglru_lambda, mix_norm_g,
              w_out, router_w, router_b, expert_w1, expert_b1, expert_w2, expert_b2, final_norm_g):
    for l in range(DEPTH):
        x = decoder_layer(x, c, norm1_g[l], norm2_g[l], w_ada[l], b_ada[l], w_in[l], conv_a_w[l],
                          sgu_w[l], sgu_b[l], sgu_ln_g[l], sgu_ln_b[l], conv_c_w[l], conv_c_b[l],
                          rglru_gate_w[l], rglru_gate_b[l], rglru_lambda[l], mix_norm_g[l], w_out[l],
                          router_w[l], router_b[l], expert_w1[l], expert_b1[l], expert_w2[l],
                          expert_b2[l])
    return rms_norm(x, final_norm_g)
```

```python
import functools

import jax
import jax.numpy as jnp
from jax import lax
from jax.experimental import pallas as pl
from jax.experimental.pallas import tpu as pltpu

F32, BF16, I32 = jnp.float32, jnp.bfloat16, jnp.int32

HEAD_DIM = 64
CHUNK = 128
RGLRU_C = 8.0
TOP_K = 4
SWIGLU_LIMIT = 7.0
SWIGLU_ALPHA = 1.702
N_MOD = 6
EPS = 1e-6

SUBLANES = 8
LANES = 128
VMEM_LIMIT_BYTES = 56 * 1024 * 1024

MIX_ROWS = 256
ROUTE_ROWS = 256
EXPERT_ROWS = 256
MOD_COLS = 1024


def _rms_normalize(y):
    return y * lax.rsqrt(jnp.mean(y * y, axis=-1, keepdims=True) + EPS)


def _mod_kernel(c_ref, w_ref, b_ref, o_ref):
    o_ref[...] = (
        jnp.dot(c_ref[...], w_ref[...], preferred_element_type=F32, precision=lax.Precision.HIGHEST)
        + b_ref[...]
    )


def _adaln_modulation(c, w_ada, b_ada):
    depth, d, n = w_ada.shape
    bsz = c.shape[0]
    rows = -(-bsz // SUBLANES) * SUBLANES
    c_pad = jnp.pad(c, ((0, rows - bsz), (0, 0)))
    out = pl.pallas_call(
        _mod_kernel,
        grid=(depth, n // MOD_COLS),
        in_specs=[
            pl.BlockSpec((rows, d), lambda l, j: (0, 0)),
            pl.BlockSpec((None, d, MOD_COLS), lambda l, j: (l, 0, j)),
            pl.BlockSpec((None, 1, MOD_COLS), lambda l, j: (l, 0, j)),
        ],
        out_specs=pl.BlockSpec((None, rows, MOD_COLS), lambda l, j: (l, 0, j)),
        out_shape=jax.ShapeDtypeStruct((depth, rows, n), F32),
        name="adaln_mod",
    )(c_pad, w_ada, b_ada.reshape(depth, 1, n))
    return out[:, :bsz].reshape(depth, bsz, N_MOD, d)


def _mixer_kernel(x_ref, mod_ref, n1g_ref, win_ref, caw_ref, sguw_ref, sgub_ref, lng_ref, lnb_ref,
                  ccw_ref, ccb_ref, gw_ref, gb_ref, lam_ref, mng_ref, wout_ref,
                  o_ref, cvh_ref, zxh_ref, hc_ref, a_scr, b_scr, *, w_conv, w_sgu, w_lru):
    rows = x_ref.shape[0]
    hist = SUBLANES

    @pl.when(pl.program_id(1) == 0)
    def _():
        cvh_ref[0:hist, :] = jnp.zeros((hist, w_conv), F32)
        zxh_ref[0:hist, :] = jnp.zeros((hist, w_lru), F32)
        hc_ref[...] = jnp.zeros_like(hc_ref)

    x = x_ref[...]
    shift1, scale1, gate1 = mod_ref[0:1, :], mod_ref[1:2, :], mod_ref[2:3, :]
    h = _rms_normalize(x) * n1g_ref[...]
    h = h * (1.0 + scale1) + shift1
    z = jnp.dot(h.astype(BF16), win_ref[...], preferred_element_type=F32)

    o = 0
    z_b = z[:, o:o + w_conv]; o += w_conv
    z_c = z[:, o:o + w_conv]; o += w_conv
    z_v = z[:, o:o + w_conv]; o += w_conv
    z_su = z[:, o:o + w_sgu]; o += w_sgu
    z_sv = z[:, o:o + w_sgu]; o += w_sgu
    z_g = z[:, o:o + w_lru]; o += w_lru
    z_x = z[:, o:o + w_lru]

    cv = z_c * z_v
    cvh_ref[hist:hist + rows, :] = cv
    conv = (caw_ref[0:1, :] * cvh_ref[hist - 2:hist - 2 + rows, :]
            + caw_ref[1:2, :] * cvh_ref[hist - 1:hist - 1 + rows, :]
            + caw_ref[2:3, :] * cv)
    cvh_ref[0:hist, :] = cvh_ref[rows:rows + hist, :]
    y_conv = z_b * conv

    u = jax.nn.gelu(z_su)
    v = jax.nn.gelu(z_sv)
    vc = v - jnp.mean(v, axis=-1, keepdims=True)
    vn = vc * lax.rsqrt(jnp.mean(vc * vc, axis=-1, keepdims=True) + EPS)
    vn = (vn * lng_ref[...] + lnb_ref[...]).astype(BF16)
    n_heads = w_sgu // HEAD_DIM
    wr = lax.broadcasted_iota(I32, (n_heads * CHUNK, CHUNK), 0)
    wc = lax.broadcasted_iota(I32, (n_heads * CHUNK, CHUNK), 1)
    w_causal = jnp.where(wc <= (wr & (CHUNK - 1)), sguw_ref[...], 0.0).astype(BF16)
    lane = lax.broadcasted_iota(I32, (CHUNK, w_sgu), 1)
    mixed = []
    for ci in range(rows // CHUNK):
        res = jnp.dot(w_causal, vn[ci * CHUNK:(ci + 1) * CHUNK, :], preferred_element_type=F32)
        m = res[0:CHUNK, :]
        for hh in range(1, n_heads):
            m = jnp.where(lane >= hh * HEAD_DIM, res[hh * CHUNK:(hh + 1) * CHUNK, :], m)
        mixed.append(m + sgub_ref[...])
    y_sgu = u * jnp.concatenate(mixed, axis=0)

    zxh_ref[hist:hist + rows, :] = z_x
    xr = (ccw_ref[0:1, :] * zxh_ref[hist - 3:hist - 3 + rows, :]
          + ccw_ref[1:2, :] * zxh_ref[hist - 2:hist - 2 + rows, :]
          + ccw_ref[2:3, :] * zxh_ref[hist - 1:hist - 1 + rows, :]
          + ccw_ref[3:4, :] * z_x + ccb_ref[...])
    zxh_ref[0:hist, :] = zxh_ref[rows:rows + hist, :]
    xr_b = xr.astype(BF16)
    half = w_lru // 2
    g0 = jnp.dot(xr_b[:, :half], gw_ref[0], preferred_element_type=F32)
    g1 = jnp.dot(xr_b[:, half:], gw_ref[1], preferred_element_type=F32)
    r_gate = jax.nn.sigmoid(jnp.concatenate([g0[:, :half], g1[:, :half]], axis=-1) + gb_ref[0:1, :])
    i_gate = jax.nn.sigmoid(jnp.concatenate([g0[:, half:], g1[:, half:]], axis=-1) + gb_ref[1:2, :])
    neg_lam = -lam_ref[...]
    softplus = jnp.maximum(neg_lam, 0.0) + jnp.log1p(jnp.exp(-jnp.abs(neg_lam)))
    log_a = (-RGLRU_C) * r_gate * softplus
    a = jnp.exp(log_a)
    a_scr[...] = a
    b_scr[...] = xr * i_gate * jnp.sqrt(1.0 - a * a)

    row8 = lax.broadcasted_iota(I32, (SUBLANES, w_lru), 0)

    def scan_group(g, carry):
        r0 = pl.multiple_of(g * SUBLANES, SUBLANES)
        a8 = a_scr[pl.ds(r0, SUBLANES), :]
        b8 = b_scr[pl.ds(r0, SUBLANES), :]
        for d in (1, 2, 4):
            keep = row8 >= d
            b8 = jnp.where(keep, a8 * pltpu.roll(b8, d, 0) + b8, b8)
            a8 = jnp.where(keep, a8 * pltpu.roll(a8, d, 0), a8)
        h8 = a8 * carry + b8
        b_scr[pl.ds(r0, SUBLANES), :] = h8
        return jnp.broadcast_to(h8[SUBLANES - 1:SUBLANES, :], (SUBLANES, w_lru))

    hc_ref[...] = lax.fori_loop(0, rows // SUBLANES, scan_group, hc_ref[...], unroll=4)
    y_lru = jax.nn.gelu(z_g) * b_scr[...]

    y = jnp.concatenate([_rms_normalize(y_conv), _rms_normalize(y_sgu), _rms_normalize(y_lru)], axis=-1)
    y = (y * mng_ref[...]).astype(BF16)
    o_ref[...] = x + gate1 * jnp.dot(y, wout_ref[...], preferred_element_type=F32)


def _mixer(x, mod_l, norm1_g, w_in, conv_a_w, sgu_w, sgu_b, sgu_ln_g, sgu_ln_b, conv_c_w, conv_c_b,
           rglru_gate_w, rglru_gate_b, rglru_lambda, mix_norm_g, w_out):
    bsz, seq, d = x.shape
    w_conv = conv_a_w.shape[1]
    w_sgu = sgu_ln_g.shape[0]
    w_lru = rglru_lambda.shape[0]
    d_in = w_in.shape[1]
    n_sgu_heads = sgu_w.shape[0]
    n_lru_heads = rglru_gate_w.shape[1]
    rows = MIX_ROWS
    half = w_lru // 2
    heads_per_half = n_lru_heads // 2

    sgu_bias = jnp.repeat(sgu_b.T, HEAD_DIM, axis=1)
    eye = jnp.eye(n_lru_heads, dtype=F32)
    bd = jnp.einsum("ghij,hk->ghikj", rglru_gate_w, eye).reshape(2, w_lru, w_lru)
    gate_w = jnp.stack([
        jnp.concatenate([bd[0, j * half:(j + 1) * half, j * half:(j + 1) * half],
                         bd[1, j * half:(j + 1) * half, j * half:(j + 1) * half]], axis=1)
        for j in range(2)]).astype(BF16)
    del heads_per_half

    full = lambda shape: pl.BlockSpec(shape, lambda b, s: (0,) * len(shape))
    kern = functools.partial(_mixer_kernel, w_conv=w_conv, w_sgu=w_sgu, w_lru=w_lru)
    return pl.pallas_call(
        kern,
        grid=(bsz, seq // rows),
        in_specs=[
            pl.BlockSpec((None, rows, d), lambda b, s: (b, s, 0)),
            pl.BlockSpec((None, N_MOD, d), lambda b, s: (b, 0, 0)),
            full((1, d)),
            full((d, d_in)),
            full(conv_a_w.shape),
            full((n_sgu_heads * CHUNK, CHUNK)),
            full((CHUNK, w_sgu)),
            full((1, w_sgu)),
            full((1, w_sgu)),
            full(conv_c_w.shape),
            full((1, w_lru)),
            full((2, half, 2 * half)),
            full((2, w_lru)),
            full((1, w_lru)),
            full((1, d)),
            full((d, d)),
        ],
        out_specs=pl.BlockSpec((None, rows, d), lambda b, s: (b, s, 0)),
        out_shape=jax.ShapeDtypeStruct((bsz, seq, d), F32),
        scratch_shapes=[
            pltpu.VMEM((rows + SUBLANES, w_conv), F32),
            pltpu.VMEM((rows + SUBLANES, w_lru), F32),
            pltpu.VMEM((SUBLANES, w_lru), F32),
            pltpu.VMEM((rows, w_lru), F32),
            pltpu.VMEM((rows, w_lru), F32),
        ],
        compiler_params=pltpu.CompilerParams(
            dimension_semantics=("arbitrary", "arbitrary"), vmem_limit_bytes=VMEM_LIMIT_BYTES),
        name="mixer",
    )(x, mod_l, norm1_g.reshape(1, d), w_in.astype(BF16), conv_a_w,
      sgu_w.reshape(n_sgu_heads * CHUNK, CHUNK), sgu_bias, sgu_ln_g.reshape(1, w_sgu),
      sgu_ln_b.reshape(1, w_sgu), conv_c_w, conv_c_b.reshape(1, w_lru), gate_w,
      rglru_gate_b.reshape(2, w_lru), rglru_lambda.reshape(1, w_lru), mix_norm_g.reshape(1, d),
      w_out.astype(BF16))


def _modulated_norm(x, g, scale, shift):
    return (_rms_normalize(x) * g) * (1.0 + scale) + shift


def _router_kernel(x_ref, mod_ref, n2g_ref, rwt_ref, rb_ref,
                   idx_ref, rank_ref, w_ref, cnt_ref, carry_ref):
    rows = x_ref.shape[0]
    n_exp = rwt_ref.shape[0]

    @pl.when(pl.program_id(0) == 0)
    def _():
        carry_ref[...] = jnp.zeros_like(carry_ref)

    h2 = _modulated_norm(x_ref[...], n2g_ref[...], mod_ref[4:5, :], mod_ref[3:4, :])
    logits = lax.dot_general(rwt_ref[...], h2, (((1,), (1,)), ((), ())),
                             preferred_element_type=F32, precision=lax.Precision.HIGHEST)
    logits = logits + rb_ref[...]

    e_iota = lax.broadcasted_iota(I32, (n_exp, rows), 0)
    work = logits
    tops, hots = [], []
    for k in range(TOP_K):
        m = jnp.max(work, axis=0, keepdims=True)
        idx = jnp.min(jnp.where(work == m, e_iota, n_exp), axis=0, keepdims=True)
        hot = e_iota == idx
        work = jnp.where(hot, -jnp.inf, work)
        tops.append(m)
        hots.append(hot)
        idx_ref[k:k + 1, :] = idx

    p = [jnp.exp(t - tops[0]) for t in tops]
    denom = p[0] + p[1] + p[2] + p[3]
    for k in range(TOP_K):
        w_ref[k:k + 1, :] = p[k] / denom

    sel = jnp.zeros((n_exp, rows), F32)
    for hot in hots:
        sel = sel + hot.astype(F32)
    s_io = lax.broadcasted_iota(I32, (rows, rows), 0)
    t_io = lax.broadcasted_iota(I32, (rows, rows), 1)
    before = (s_io < t_io).astype(BF16)
    rank_e = jnp.dot(sel.astype(BF16), before, preferred_element_type=F32) + carry_ref[:, 0:1]
    for k in range(TOP_K):
        rank_k = jnp.sum(jnp.where(hots[k], rank_e, 0.0), axis=0, keepdims=True)
        rank_ref[k:k + 1, :] = rank_k.astype(I32)
    carry_ref[...] = carry_ref[...] + jnp.sum(sel, axis=1, keepdims=True)
    cnt_ref[...] = carry_ref[...].astype(I32)


def _router(x1, mod_l, norm2_g, router_w, router_b):
    bsz, seq, d = x1.shape
    n_tok = bsz * seq
    n_exp = router_w.shape[1]
    rows = ROUTE_ROWS
    tiles_per_seq = seq // rows
    tok_spec = pl.BlockSpec((TOP_K, rows), lambda i: (0, i))
    return pl.pallas_call(
        _router_kernel,
        grid=(n_tok // rows,),
        in_specs=[
            pl.BlockSpec((rows, d), lambda i: (i, 0)),
            pl.BlockSpec((None, N_MOD, d), lambda i: (i // tiles_per_seq, 0, 0)),
            pl.BlockSpec((1, d), lambda i: (0, 0)),
            pl.BlockSpec((n_exp, d), lambda i: (0, 0)),
            pl.BlockSpec((n_exp, 1), lambda i: (0, 0)),
        ],
        out_specs=[tok_spec, tok_spec, tok_spec, pl.BlockSpec((n_exp, LANES), lambda i: (0, 0))],
        out_shape=[
            jax.ShapeDtypeStruct((TOP_K, n_tok), I32),
            jax.ShapeDtypeStruct((TOP_K, n_tok), I32),
            jax.ShapeDtypeStruct((TOP_K, n_tok), F32),
            jax.ShapeDtypeStruct((n_exp, LANES), I32),
        ],
        scratch_shapes=[pltpu.VMEM((n_exp, LANES), F32)],
        compiler_params=pltpu.CompilerParams(
            dimension_semantics=("arbitrary",), vmem_limit_bytes=VMEM_LIMIT_BYTES),
        name="router",
    )(x1.reshape(n_tok, d), mod_l, norm2_g.reshape(1, d), router_w.T, router_b.reshape(n_exp, 1))


def _dispatch_kernel(cnt_sm, base_sm, nblk_sm, x_ref, mod_ref, n2g_ref, idx_ref, rank_ref, base_ref,
                     slot_ref, xs_ref, h_buf, zero_buf, slot_vm, slot_sm, row_sem, fill_sem, sm_sem,
                     *, n_blocks):
    rows = x_ref.shape[0]
    n_exp = base_ref.shape[0]
    step = pl.program_id(0)
    n_steps = pl.num_programs(0)
    par = step % 2

    def row_copy(buf, t, slot):
        return pltpu.make_async_copy(h_buf.at[buf, pl.ds(t, 1), :], xs_ref.at[pl.ds(slot, 1), :],
                                     row_sem.at[buf])

    def wait_rows(buf):
        def body(t, c):
            row_copy(buf, 0, 0).wait()
            return c
        lax.fori_loop(0, TOP_K * rows, body, 0)

    def pad_copy(slot):
        return pltpu.make_async_copy(zero_buf.at[pl.ds(0, 1), :], xs_ref.at[pl.ds(slot, 1), :], fill_sem)

    def tail_copy(blk):
        r0 = pl.multiple_of(blk * EXPERT_ROWS, EXPERT_ROWS)
        return pltpu.make_async_copy(zero_buf, xs_ref.at[pl.ds(r0, EXPERT_ROWS), :], fill_sem)

    def fill(start):
        def per_expert(e, c):
            lo = base_sm[e] + cnt_sm[e]
            hi = base_sm[e] + (cnt_sm[e] + EXPERT_ROWS - 1) // EXPERT_ROWS * EXPERT_ROWS

            def per_row(r, c2):
                if start:
                    pad_copy(r).start()
                else:
                    pad_copy(r).wait()
                return c2
            return lax.fori_loop(lo, hi, per_row, c)
        lax.fori_loop(0, n_exp, per_expert, 0)

        def per_block(b, c):
            if start:
                tail_copy(b).start()
            else:
                tail_copy(b).wait()
            return c
        lax.fori_loop(nblk_sm[0], n_blocks, per_block, 0)

    @pl.when(step == 0)
    def _():
        zero_buf[...] = jnp.zeros_like(zero_buf)
        fill(True)

    @pl.when(step >= 2)
    def _():
        wait_rows(par)

    h_buf[par] = _modulated_norm(x_ref[...], n2g_ref[...], mod_ref[4:5, :], mod_ref[3:4, :])

    e_iota = lax.broadcasted_iota(I32, (n_exp, rows), 0)
    for k in range(TOP_K):
        hot = e_iota == idx_ref[k:k + 1, :]
        slot_k = jnp.sum(jnp.where(hot, base_ref[...], 0), axis=0, keepdims=True) + rank_ref[k:k + 1, :]
        slot_vm[k:k + 1, :] = slot_k
    slot_ref[...] = slot_vm[...]
    to_smem = pltpu.make_async_copy(slot_vm, slot_sm, sm_sem)
    to_smem.start()
    to_smem.wait()

    def issue(t, c):
        for k in range(TOP_K):
            row_copy(par, t, slot_sm[k, t]).start()
        return c
    lax.fori_loop(0, rows, issue, 0)

    @pl.when(step == n_steps - 1)
    def _():
        wait_rows(par)

        @pl.when(n_steps >= 2)
        def _():
            wait_rows(1 - par)
        fill(False)


def _dispatch(x1, mod_l, norm2_g, idx, rank, counts, base, n_active_blocks, n_blocks):
    bsz, seq, d = x1.shape
    n_tok = bsz * seq
    n_exp = counts.shape[0]
    rows = ROUTE_ROWS
    tiles_per_seq = seq // rows
    tok_spec = pl.BlockSpec((TOP_K, rows), lambda i, *_: (0, i))
    grid_spec = pltpu.PrefetchScalarGridSpec(
        num_scalar_prefetch=3,
        grid=(n_tok // rows,),
        in_specs=[
            pl.BlockSpec((rows, d), lambda i, *_: (i, 0)),
            pl.BlockSpec((None, N_MOD, d), lambda i, *_: (i // tiles_per_seq, 0, 0)),
            pl.BlockSpec((1, d), lambda i, *_: (0, 0)),
            tok_spec,
            tok_spec,
            pl.BlockSpec((n_exp, 1), lambda i, *_: (0, 0)),
        ],
        out_specs=[tok_spec, pl.BlockSpec(memory_space=pl.ANY)],
        scratch_shapes=[
            pltpu.VMEM((2, rows, d), F32),
            pltpu.VMEM((EXPERT_ROWS, d), F32),
            pltpu.VMEM((TOP_K, rows), I32),
            pltpu.SMEM((TOP_K, rows), I32),
            pltpu.SemaphoreType.DMA((2,)),
            pltpu.SemaphoreType.DMA,
            pltpu.SemaphoreType.DMA,
        ],
    )
    return pl.pallas_call(
        functools.partial(_dispatch_kernel, n_blocks=n_blocks),
        grid_spec=grid_spec,
        out_shape=[
            jax.ShapeDtypeStruct((TOP_K, n_tok), I32),
            jax.ShapeDtypeStruct((n_blocks * EXPERT_ROWS, d), F32),
        ],
        compiler_params=pltpu.CompilerParams(
            dimension_semantics=("arbitrary",), vmem_limit_bytes=VMEM_LIMIT_BYTES),
        name="dispatch",
    )(counts, base, n_active_blocks, x1.reshape(n_tok, d), mod_l, norm2_g.reshape(1, d), idx, rank,
      base.reshape(n_exp, 1))


def _expert_kernel(blk_e_sm, first_sm, nact_sm, xs_ref, w1_ref, b1_ref, w2_ref, b2_ref, ys_ref,
                   w1b_ref, w2b_ref):
    g = pl.program_id(0)
    d_ff = w2_ref.shape[0]

    @pl.when(first_sm[g] == 1)
    def _():
        w1b_ref[...] = w1_ref[...].astype(BF16)
        w2b_ref[...] = w2_ref[...].astype(BF16)

    @pl.when(g < nact_sm[0])
    def _():
        hgu = jnp.dot(xs_ref[...].astype(BF16), w1b_ref[...], preferred_element_type=F32) + b1_ref[...]
        gt = jnp.minimum(hgu[:, :d_ff], SWIGLU_LIMIT)
        up = jnp.clip(hgu[:, d_ff:], -SWIGLU_LIMIT, SWIGLU_LIMIT)
        act = (up + 1.0) * (gt * jax.nn.sigmoid(SWIGLU_ALPHA * gt))
        ys_ref[...] = jnp.dot(act.astype(BF16), w2b_ref[...], preferred_element_type=F32) + b2_ref[...]

    @pl.when(g >= nact_sm[0])
    def _():
        ys_ref[...] = jnp.zeros_like(ys_ref)


def _experts(xs, block_e, first, n_active_blocks, w1, b1, w2, b2):
    n_slots, d = xs.shape
    n_exp, _, d_hid = w1.shape
    d_ff = w2.shape[1]
    n_blocks = n_slots // EXPERT_ROWS
    grid_spec = pltpu.PrefetchScalarGridSpec(
        num_scalar_prefetch=3,
        grid=(n_blocks,),
        in_specs=[
            pl.BlockSpec((EXPERT_ROWS, d), lambda g, be, fi, na: (g, 0)),
            pl.BlockSpec((None, d, d_hid), lambda g, be, fi, na: (be[g], 0, 0)),
            pl.BlockSpec((None, 1, d_hid), lambda g, be, fi, na: (be[g], 0, 0)),
            pl.BlockSpec((None, d_ff, d), lambda g, be, fi, na: (be[g], 0, 0)),
            pl.BlockSpec((None, 1, d), lambda g, be, fi, na: (be[g], 0, 0)),
        ],
        out_specs=pl.BlockSpec((EXPERT_ROWS, d), lambda g, be, fi, na: (g, 0)),
        scratch_shapes=[pltpu.VMEM((d, d_hid), BF16), pltpu.VMEM((d_ff, d), BF16)],
    )
    return pl.pallas_call(
        _expert_kernel,
        grid_spec=grid_spec,
        out_shape=jax.ShapeDtypeStruct((n_slots, d), F32),
        compiler_params=pltpu.CompilerParams(
            dimension_semantics=("arbitrary",), vmem_limit_bytes=VMEM_LIMIT_BYTES),
        name="experts",
    )(block_e, first, n_active_blocks, xs, w1, b1.reshape(n_exp, 1, d_hid), w2, b2.reshape(n_exp, 1, d))


def _combine_kernel(x_ref, mod_ref, slot_ref, slot_nxt_ref, w_ref, fng_ref, ys_ref, o_ref,
                    y_buf, slot_sm, row_sem, sm_sem, *, final_norm):
    rows = x_ref.shape[0]
    step = pl.program_id(0)
    n_steps = pl.num_programs(0)
    par = step % 2

    def row_copy(buf, k, t, slot):
        return pltpu.make_async_copy(ys_ref.at[pl.ds(slot, 1), :], y_buf.at[buf, k, pl.ds(t, 1), :],
                                     row_sem.at[buf])

    def gather(src_ref, buf):
        to_smem = pltpu.make_async_copy(src_ref, slot_sm, sm_sem)
        to_smem.start()
        to_smem.wait()

        def issue(t, c):
            for k in range(TOP_K):
                row_copy(buf, k, t, slot_sm[k, t]).start()
            return c
        lax.fori_loop(0, rows, issue, 0)

    @pl.when(step == 0)
    def _():
        gather(slot_ref, 0)

    @pl.when(step + 1 < n_steps)
    def _():
        gather(slot_nxt_ref, 1 - par)

    def wait_one(t, c):
        row_copy(par, 0, 0, 0).wait()
        return c
    lax.fori_loop(0, TOP_K * rows, wait_one, 0)

    w_t = jnp.transpose(jnp.concatenate([w_ref[...], jnp.zeros((LANES - TOP_K, rows), F32)], axis=0))
    moe = w_t[:, 0:1] * y_buf[par, 0]
    for k in range(1, TOP_K):
        moe = moe + w_t[:, k:k + 1] * y_buf[par, k]
    out = x_ref[...] + mod_ref[5:6, :] * moe
    if final_norm:
        out = _rms_normalize(out) * fng_ref[...]
    o_ref[...] = out


def _combine(x1, mod_l, slot, top_w, ys, final_norm_g, final_norm):
    bsz, seq, d = x1.shape
    n_tok = bsz * seq
    rows = ROUTE_ROWS
    tiles_per_seq = seq // rows
    n_steps = n_tok // rows
    tok_spec = pl.BlockSpec((TOP_K, rows), lambda i: (0, i))
    nxt_spec = pl.BlockSpec((TOP_K, rows), lambda i: (0, jnp.minimum(i + 1, n_steps - 1)))
    out = pl.pallas_call(
        functools.partial(_combine_kernel, final_norm=final_norm),
        grid=(n_steps,),
        in_specs=[
            pl.BlockSpec((rows, d), lambda i: (i, 0)),
            pl.BlockSpec((None, N_MOD, d), lambda i: (i // tiles_per_seq, 0, 0)),
            tok_spec,
            nxt_spec,
            tok_spec,
            pl.BlockSpec((1, d), lambda i: (0, 0)),
            pl.BlockSpec(memory_space=pl.ANY),
        ],
        out_specs=pl.BlockSpec((rows, d), lambda i: (i, 0)),
        out_shape=jax.ShapeDtypeStruct((n_tok, d), F32),
        scratch_shapes=[
            pltpu.VMEM((2, TOP_K, rows, d), F32),
            pltpu.SMEM((TOP_K, rows), I32),
            pltpu.SemaphoreType.DMA((2,)),
            pltpu.SemaphoreType.DMA,
        ],
        compiler_params=pltpu.CompilerParams(
            dimension_semantics=("arbitrary",), vmem_limit_bytes=VMEM_LIMIT_BYTES),
        name="combine",
    )(x1.reshape(n_tok, d), mod_l, slot, slot, top_w, final_norm_g.reshape(1, d), ys)
    return out.reshape(bsz, seq, d)


def _moe_layer(x1, mod_l, norm2_g, router_w, router_b, w1, b1, w2, b2, final_norm_g, final_norm):
    bsz, seq, _ = x1.shape
    n_tok = bsz * seq
    n_exp = router_w.shape[1]
    idx, rank, top_w, cnt = _router(x1, mod_l, norm2_g, router_w, router_b)

    counts = cnt[:, 0]
    padded = (counts + EXPERT_ROWS - 1) // EXPERT_ROWS * EXPERT_ROWS
    pad_end = jnp.cumsum(padded)
    base = (pad_end - padded).astype(I32)
    n_blocks = (n_tok * TOP_K) // EXPERT_ROWS + n_exp
    n_active = (pad_end[-1] // EXPERT_ROWS).astype(I32)
    blk_row = jnp.minimum(jnp.arange(n_blocks, dtype=I32), n_active - 1) * EXPERT_ROWS
    block_e = jnp.minimum(jnp.searchsorted(pad_end, blk_row, side="right"), n_exp - 1).astype(I32)
    first = ((blk_row == base[block_e]) & (jnp.arange(n_blocks) < n_active)).astype(I32)
    n_active = n_active.reshape(1)

    slot, xs = _dispatch(x1, mod_l, norm2_g, idx, rank, counts, base, n_active, n_blocks)
    ys = _experts(xs, block_e, first, n_active, w1, b1, w2, b2)
    return _combine(x1, mod_l, slot, top_w, ys, final_norm_g, final_norm)


def kernel(x, c, norm1_g, norm2_g, w_ada, b_ada, w_in, conv_a_w, sgu_w, sgu_b, sgu_ln_g, sgu_ln_b,
           conv_c_w, conv_c_b, rglru_gate_w, rglru_gate_b, rglru_lambda, mix_norm_g, w_out, router_w,
           router_b, expert_w1, expert_b1, expert_w2, expert_b2, final_norm_g):
    depth = w_ada.shape[0]
    mod = _adaln_modulation(c, w_ada, b_ada)
    for l in range(depth):
        x1 = _mixer(x, mod[l], norm1_g[l], w_in[l], conv_a_w[l], sgu_w[l], sgu_b[l], sgu_ln_g[l],
                    sgu_ln_b[l], conv_c_w[l], conv_c_b[l], rglru_gate_w[l], rglru_gate_b[l],
                    rglru_lambda[l], mix_norm_g[l], w_out[l])
        x = _moe_layer(x1, mod[l], norm2_g[l], router_w[l], router_b[l], expert_w1[l], expert_b1[l],
                       expert_w2[l], expert_b2[l], final_norm_g, l == depth - 1)
    return x
```

```python
import functools

import jax
import jax.numpy as jnp
from jax import lax
from jax.experimental import pallas as pl
from jax.experimental.pallas import tpu as pltpu

F32, BF16, I32 = jnp.float32, jnp.bfloat16, jnp.int32

HEAD_DIM = 64
CHUNK = 128
RGLRU_C = 8.0
TOP_K = 4
SWIGLU_LIMIT = 7.0
SWIGLU_ALPHA = 1.702
N_MOD = 6
EPS = 1e-6

SUBLANES = 8
LANES = 128
VMEM_LIMIT_BYTES = 56 * 1024 * 1024

MIX_ROWS = 256
ROUTE_ROWS = 256
EXPERT_ROWS = 256
MOD_COLS = 1024


def _rms_normalize(y):
    return y * lax.rsqrt(jnp.mean(y * y, axis=-1, keepdims=True) + EPS)


def _mod_kernel(c_ref, w_ref, b_ref, o_ref):
    o_ref[...] = (
        jnp.dot(c_ref[...], w_ref[...], preferred_element_type=F32, precision=lax.Precision.HIGHEST)
        + b_ref[...]
    )


def _adaln_modulation(c, w_ada, b_ada):
    depth, d, n = w_ada.shape
    bsz = c.shape[0]
    rows = -(-bsz // SUBLANES) * SUBLANES
    c_pad = jnp.pad(c, ((0, rows - bsz), (0, 0)))
    out = pl.pallas_call(
        _mod_kernel,
        grid=(depth, n // MOD_COLS),
        in_specs=[
            pl.BlockSpec((rows, d), lambda l, j: (0, 0)),
            pl.BlockSpec((None, d, MOD_COLS), lambda l, j: (l, 0, j)),
            pl.BlockSpec((None, 1, MOD_COLS), lambda l, j: (l, 0, j)),
        ],
        out_specs=pl.BlockSpec((None, rows, MOD_COLS), lambda l, j: (l, 0, j)),
        out_shape=jax.ShapeDtypeStruct((depth, rows, n), F32),
        name="adaln_mod",
    )(c_pad, w_ada, b_ada.reshape(depth, 1, n))
    return out[:, :bsz].reshape(depth, bsz, N_MOD, d)


def _mixer_kernel(x_ref, mod_ref, n1g_ref, win_ref, caw_ref, sguw_ref, sgub_ref, lng_ref, lnb_ref,
                  ccw_ref, ccb_ref, gw_ref, gb_ref, lam_ref, mng_ref, wout_ref,
                  o_ref, cvh_ref, zxh_ref, hc_ref, a_scr, b_scr, *, w_conv, w_sgu, w_lru):
    rows = x_ref.shape[0]
    hist = SUBLANES

    @pl.when(pl.program_id(1) == 0)
    def _():
        cvh_ref[0:hist, :] = jnp.zeros((hist, w_conv), F32)
        zxh_ref[0:hist, :] = jnp.zeros((hist, w_lru), F32)
        hc_ref[...] = jnp.zeros_like(hc_ref)

    x = x_ref[...]
    shift1, scale1, gate1 = mod_ref[0:1, :], mod_ref[1:2, :], mod_ref[2:3, :]
    h = _rms_normalize(x) * n1g_ref[...]
    h = h * (1.0 + scale1) + shift1
    z = jnp.dot(h.astype(BF16), win_ref[...], preferred_element_type=F32)

    o = 0
    z_b = z[:, o:o + w_conv]; o += w_conv
    z_c = z[:, o:o + w_conv]; o += w_conv
    z_v = z[:, o:o + w_conv]; o += w_conv
    z_su = z[:, o:o + w_sgu]; o += w_sgu
    z_sv = z[:, o:o + w_sgu]; o += w_sgu
    z_g = z[:, o:o + w_lru]; o += w_lru
    z_x = z[:, o:o + w_lru]

    cv = z_c * z_v
    cvh_ref[hist:hist + rows, :] = cv
    conv = (caw_ref[0:1, :] * cvh_ref[hist - 2:hist - 2 + rows, :]
            + caw_ref[1:2, :] * cvh_ref[hist - 1:hist - 1 + rows, :]
            + caw_ref[2:3, :] * cv)
    cvh_ref[0:hist, :] = cvh_ref[rows:rows + hist, :]
    y_conv = z_b * conv

    u = jax.nn.gelu(z_su)
    v = jax.nn.gelu(z_sv)
    vc = v - jnp.mean(v, axis=-1, keepdims=True)
    vn = vc * lax.rsqrt(jnp.mean(vc * vc, axis=-1, keepdims=True) + EPS)
    vn = (vn * lng_ref[...] + lnb_ref[...]).astype(BF16)
    n_heads = w_sgu // HEAD_DIM
    wr = lax.broadcasted_iota(I32, (n_heads * CHUNK, CHUNK), 0)
    wc = lax.broadcasted_iota(I32, (n_heads * CHUNK, CHUNK), 1)
    w_causal = jnp.where(wc <= (wr & (CHUNK - 1)), sguw_ref[...], 0.0).astype(BF16)
    lane = lax.broadcasted_iota(I32, (CHUNK, w_sgu), 1)
    mixed = []
    for ci in range(rows // CHUNK):
        res = jnp.dot(w_causal, vn[ci * CHUNK:(ci + 1) * CHUNK, :], preferred_element_type=F32)
        m = res[0:CHUNK, :]
        for hh in range(1, n_heads):
            m = jnp.where(lane >= hh * HEAD_DIM, res[hh * CHUNK:(hh + 1) * CHUNK, :], m)
        mixed.append(m + sgub_ref[...])
    y_sgu = u * jnp.concatenate(mixed, axis=0)

    zxh_ref[hist:hist + rows, :] = z_x
    xr = (ccw_ref[0:1, :] * zxh_ref[hist - 3:hist - 3 + rows, :]
          + ccw_ref[1:2, :] * zxh_ref[hist - 2:hist - 2 + rows, :]
          + ccw_ref[2:3, :] * zxh_ref[hist - 1:hist - 1 + rows, :]
          + ccw_ref[3:4, :] * z_x + ccb_ref[...])
    zxh_ref[0:hist, :] = zxh_ref[rows:rows + hist, :]
    xr_b = xr.astype(BF16)
    half = w_lru // 2
    g0 = jnp.dot(xr_b[:, :half], gw_ref[0], preferred_element_type=F32)
    g1 = jnp.dot(xr_b[:, half:], gw_ref[1], preferred_element_type=F32)
    r_gate = jax.nn.sigmoid(jnp.concatenate([g0[:, :half], g1[:, :half]], axis=-1) + gb_ref[0:1, :])
    i_gate = jax.nn.sigmoid(jnp.concatenate([g0[:, half:], g1[:, half:]], axis=-1) + gb_ref[1:2, :])
    neg_lam = -lam_ref[...]
    softplus = jnp.maximum(neg_lam, 0.0) + jnp.log1p(jnp.exp(-jnp.abs(neg_lam)))
    log_a = (-RGLRU_C) * r_gate * softplus
    a = jnp.exp(log_a)
    a_scr[...] = a
    b_scr[...] = xr * i_gate * jnp.sqrt(1.0 - a * a)

    row8 = lax.broadcasted_iota(I32, (SUBLANES, w_lru), 0)

    def scan_group(g, carry):
        r0 = pl.multiple_of(g * SUBLANES, SUBLANES)
        a8 = a_scr[pl.ds(r0, SUBLANES), :]
        b8 = b_scr[pl.ds(r0, SUBLANES), :]
        for d in (1, 2, 4):
            keep = row8 >= d
            b8 = jnp.where(keep, a8 * pltpu.roll(b8, d, 0) + b8, b8)
            a8 = jnp.where(keep, a8 * pltpu.roll(a8, d, 0), a8)
        h8 = a8 * carry + b8
        b_scr[pl.ds(r0, SUBLANES), :] = h8
        return jnp.broadcast_to(h8[SUBLANES - 1:SUBLANES, :], (SUBLANES, w_lru))

    hc_ref[...] = lax.fori_loop(0, rows // SUBLANES, scan_group, hc_ref[...], unroll=4)
    y_lru = jax.nn.gelu(z_g) * b_scr[...]

    y = jnp.concatenate([_rms_normalize(y_conv), _rms_normalize(y_sgu), _rms_normalize(y_lru)], axis=-1)
    y = (y * mng_ref[...]).astype(BF16)
    o_ref[...] = x + gate1 * jnp.dot(y, wout_ref[...], preferred_element_type=F32)


def _mixer(x, mod_l, norm1_g, w_in, conv_a_w, sgu_w, sgu_b, sgu_ln_g, sgu_ln_b, conv_c_w, conv_c_b,
           rglru_gate_w, rglru_gate_b, rglru_lambda, mix_norm_g, w_out):
    bsz, seq, d = x.shape
    w_conv = conv_a_w.shape[1]
    w_sgu = sgu_ln_g.shape[0]
    w_lru = rglru_lambda.shape[0]
    d_in = w_in.shape[1]
    n_sgu_heads = sgu_w.shape[0]
    n_lru_heads = rglru_gate_w.shape[1]
    rows = MIX_ROWS
    half = w_lru // 2
    heads_per_half = n_lru_heads // 2

    sgu_bias = jnp.repeat(sgu_b.T, HEAD_DIM, axis=1)
    eye = jnp.eye(n_lru_heads, dtype=F32)
    bd = jnp.einsum("ghij,hk->ghikj", rglru_gate_w, eye).reshape(2, w_lru, w_lru)
    gate_w = jnp.stack([
        jnp.concatenate([bd[0, j * half:(j + 1) * half, j * half:(j + 1) * half],
                         bd[1, j * half:(j + 1) * half, j * half:(j + 1) * half]], axis=1)
        for j in range(2)]).astype(BF16)
    del heads_per_half

    full = lambda shape: pl.BlockSpec(shape, lambda b, s: (0,) * len(shape))
    kern = functools.partial(_mixer_kernel, w_conv=w_conv, w_sgu=w_sgu, w_lru=w_lru)
    return pl.pallas_call(
        kern,
        grid=(bsz, seq // rows),
        in_specs=[
            pl.BlockSpec((None, rows, d), lambda b, s: (b, s, 0)),
            pl.BlockSpec((None, N_MOD, d), lambda b, s: (b, 0, 0)),
            full((1, d)),
            full((d, d_in)),
            full(conv_a_w.shape),
            full((n_sgu_heads * CHUNK, CHUNK)),
            full((CHUNK, w_sgu)),
            full((1, w_sgu)),
            full((1, w_sgu)),
            full(conv_c_w.shape),
            full((1, w_lru)),
            full((2, half, 2 * half)),
            full((2, w_lru)),
            full((1, w_lru)),
            full((1, d)),
            full((d, d)),
        ],
        out_specs=pl.BlockSpec((None, rows, d), lambda b, s: (b, s, 0)),
        out_shape=jax.ShapeDtypeStruct((bsz, seq, d), F32),
        scratch_shapes=[
            pltpu.VMEM((rows + SUBLANES, w_conv), F32),
            pltpu.VMEM((rows + SUBLANES, w_lru), F32),
            pltpu.VMEM((SUBLANES, w_lru), F32),
            pltpu.VMEM((rows, w_lru), F32),
            pltpu.VMEM((rows, w_lru), F32),
        ],
        compiler_params=pltpu.CompilerParams(
            dimension_semantics=("arbitrary", "arbitrary"), vmem_limit_bytes=VMEM_LIMIT_BYTES),
        name="mixer",
    )(x, mod_l, norm1_g.reshape(1, d), w_in.astype(BF16), conv_a_w,
      sgu_w.reshape(n_sgu_heads * CHUNK, CHUNK), sgu_bias, sgu_ln_g.reshape(1, w_sgu),
      sgu_ln_b.reshape(1, w_sgu), conv_c_w, conv_c_b.reshape(1, w_lru), gate_w,
      rglru_gate_b.reshape(2, w_lru), rglru_lambda.reshape(1, w_lru), mix_norm_g.reshape(1, d),
      w_out.astype(BF16))


def _modulated_norm(x, g, scale, shift):
    return (_rms_normalize(x) * g) * (1.0 + scale) + shift


def _router_kernel(x_ref, mod_ref, n2g_ref, rwt_ref, rb_ref,
                   idx_ref, rank_ref, w_ref, cnt_ref, carry_ref):
    rows = x_ref.shape[0]
    n_exp = rwt_ref.shape[0]

    @pl.when(pl.program_id(0) == 0)
    def _():
        carry_ref[...] = jnp.zeros_like(carry_ref)

    h2 = _modulated_norm(x_ref[...], n2g_ref[...], mod_ref[4:5, :], mod_ref[3:4, :])
    logits = lax.dot_general(rwt_ref[...], h2, (((1,), (1,)), ((), ())),
                             preferred_element_type=F32, precision=lax.Precision.HIGHEST)
    logits = logits + rb_ref[...]

    e_iota = lax.broadcasted_iota(I32, (n_exp, rows), 0).astype(F32)
    work = logits
    tops, hots = [], []
    for k in range(TOP_K):
        m = jnp.max(work, axis=0, keepdims=True)
        idx = jnp.min(jnp.where(work == m, e_iota, float(n_exp)), axis=0, keepdims=True)
        hot = e_iota == idx
        work = jnp.where(hot, -jnp.inf, work)
        tops.append(m)
        hots.append(hot)
        idx_ref[k:k + 1, :] = idx.astype(I32)

    p = [jnp.exp(t - tops[0]) for t in tops]
    denom = p[0] + p[1] + p[2] + p[3]
    for k in range(TOP_K):
        w_ref[k:k + 1, :] = p[k] / denom

    sel = jnp.zeros((n_exp, rows), F32)
    for hot in hots:
        sel = sel + hot.astype(F32)
    s_io = lax.broadcasted_iota(I32, (rows, rows), 0)
    t_io = lax.broadcasted_iota(I32, (rows, rows), 1)
    before = (s_io < t_io).astype(BF16)
    rank_e = jnp.dot(sel.astype(BF16), before, preferred_element_type=F32) + carry_ref[:, 0:1]
    for k in range(TOP_K):
        rank_k = jnp.sum(jnp.where(hots[k], rank_e, 0.0), axis=0, keepdims=True)
        rank_ref[k:k + 1, :] = rank_k.astype(I32)
    carry_ref[...] = carry_ref[...] + jnp.sum(sel, axis=1, keepdims=True)
    cnt_ref[...] = carry_ref[...].astype(I32)


def _router(x1, mod_l, norm2_g, router_w, router_b):
    bsz, seq, d = x1.shape
    n_tok = bsz * seq
    n_exp = router_w.shape[1]
    rows = ROUTE_ROWS
    tiles_per_seq = seq // rows
    tok_spec = pl.BlockSpec((TOP_K, rows), lambda i: (0, i))
    return pl.pallas_call(
        _router_kernel,
        grid=(n_tok // rows,),
        in_specs=[
            pl.BlockSpec((rows, d), lambda i: (i, 0)),
            pl.BlockSpec((None, N_MOD, d), lambda i: (i // tiles_per_seq, 0, 0)),
            pl.BlockSpec((1, d), lambda i: (0, 0)),
            pl.BlockSpec((n_exp, d), lambda i: (0, 0)),
            pl.BlockSpec((n_exp, 1), lambda i: (0, 0)),
        ],
        out_specs=[tok_spec, tok_spec, tok_spec, pl.BlockSpec((n_exp, LANES), lambda i: (0, 0))],
        out_shape=[
            jax.ShapeDtypeStruct((TOP_K, n_tok), I32),
            jax.ShapeDtypeStruct((TOP_K, n_tok), I32),
            jax.ShapeDtypeStruct((TOP_K, n_tok), F32),
            jax.ShapeDtypeStruct((n_exp, LANES), I32),
        ],
        scratch_shapes=[pltpu.VMEM((n_exp, LANES), F32)],
        compiler_params=pltpu.CompilerParams(
            dimension_semantics=("arbitrary",), vmem_limit_bytes=VMEM_LIMIT_BYTES),
        name="router",
    )(x1.reshape(n_tok, d), mod_l, norm2_g.reshape(1, d), router_w.T, router_b.reshape(n_exp, 1))


def _dispatch_kernel(cnt_sm, base_sm, nblk_sm, x_ref, mod_ref, n2g_ref, idx_ref, rank_ref, base_ref,
                     slot_ref, xs_ref, h_buf, zero_buf, slot_vm, slot_sm, row_sem, fill_sem, sm_sem,
                     *, n_blocks):
    rows = x_ref.shape[0]
    n_exp = base_ref.shape[0]
    step = pl.program_id(0)
    n_steps = pl.num_programs(0)
    par = step % 2

    def row_copy(buf, t, slot):
        return pltpu.make_async_copy(h_buf.at[buf, pl.ds(t, 1), :], xs_ref.at[pl.ds(slot, 1), :],
                                     row_sem.at[buf])

    def wait_rows(buf):
        for _ in range(TOP_K):
            pltpu.make_async_copy(h_buf.at[buf], xs_ref.at[pl.ds(0, rows), :], row_sem.at[buf]).wait()

    def pad_copy(slot):
        return pltpu.make_async_copy(zero_buf.at[pl.ds(0, 1), :], xs_ref.at[pl.ds(slot, 1), :], fill_sem)

    def tail_copy(blk):
        r0 = pl.multiple_of(blk * EXPERT_ROWS, EXPERT_ROWS)
        return pltpu.make_async_copy(zero_buf, xs_ref.at[pl.ds(r0, EXPERT_ROWS), :], fill_sem)

    def fill(start):
        def per_expert(e, c):
            lo = base_sm[e] + cnt_sm[e]
            hi = base_sm[e] + (cnt_sm[e] + EXPERT_ROWS - 1) // EXPERT_ROWS * EXPERT_ROWS

            def per_row(r, c2):
                if start:
                    pad_copy(r).start()
                else:
                    pad_copy(r).wait()
                return c2
            return lax.fori_loop(lo, hi, per_row, c)
        lax.fori_loop(0, n_exp, per_expert, 0)

        def per_block(b, c):
            if start:
                tail_copy(b).start()
            else:
                tail_copy(b).wait()
            return c
        lax.fori_loop(nblk_sm[0], n_blocks, per_block, 0)

    @pl.when(step == 0)
    def _():
        zero_buf[...] = jnp.zeros_like(zero_buf)
        fill(True)

    @pl.when(step >= 2)
    def _():
        wait_rows(par)

    h_buf[par] = _modulated_norm(x_ref[...], n2g_ref[...], mod_ref[4:5, :], mod_ref[3:4, :])

    e_iota = lax.broadcasted_iota(I32, (n_exp, rows), 0)
    for k in range(TOP_K):
        hot = e_iota == idx_ref[k:k + 1, :]
        slot_k = jnp.sum(jnp.where(hot, base_ref[...], 0), axis=0, keepdims=True) + rank_ref[k:k + 1, :]
        slot_vm[k:k + 1, :] = slot_k
    slot_ref[...] = slot_vm[...]
    to_smem = pltpu.make_async_copy(slot_vm, slot_sm, sm_sem)
    to_smem.start()
    to_smem.wait()

    def issue(t, c):
        for k in range(TOP_K):
            row_copy(par, t, slot_sm[k, t]).start()
        return c
    lax.fori_loop(0, rows, issue, 0, unroll=8)

    @pl.when(step == n_steps - 1)
    def _():
        wait_rows(par)

        @pl.when(n_steps >= 2)
        def _():
            wait_rows(1 - par)
        fill(False)


def _dispatch(x1, mod_l, norm2_g, idx, rank, counts, base, n_active_blocks, n_blocks):
    bsz, seq, d = x1.shape
    n_tok = bsz * seq
    n_exp = counts.shape[0]
    rows = ROUTE_ROWS
    tiles_per_seq = seq // rows
    tok_spec = pl.BlockSpec((TOP_K, rows), lambda i, *_: (0, i))
    grid_spec = pltpu.PrefetchScalarGridSpec(
        num_scalar_prefetch=3,
        grid=(n_tok // rows,),
        in_specs=[
            pl.BlockSpec((rows, d), lambda i, *_: (i, 0)),
            pl.BlockSpec((None, N_MOD, d), lambda i, *_: (i // tiles_per_seq, 0, 0)),
            pl.BlockSpec((1, d), lambda i, *_: (0, 0)),
            tok_spec,
            tok_spec,
            pl.BlockSpec((n_exp, 1), lambda i, *_: (0, 0)),
        ],
        out_specs=[tok_spec, pl.BlockSpec(memory_space=pl.ANY)],
        scratch_shapes=[
            pltpu.VMEM((2, rows, d), F32),
            pltpu.VMEM((EXPERT_ROWS, d), F32),
            pltpu.VMEM((TOP_K, rows), I32),
            pltpu.SMEM((TOP_K, rows), I32),
            pltpu.SemaphoreType.DMA((2,)),
            pltpu.SemaphoreType.DMA,
            pltpu.SemaphoreType.DMA,
        ],
    )
    return pl.pallas_call(
        functools.partial(_dispatch_kernel, n_blocks=n_blocks),
        grid_spec=grid_spec,
        out_shape=[
            jax.ShapeDtypeStruct((TOP_K, n_tok), I32),
            jax.ShapeDtypeStruct((n_blocks * EXPERT_ROWS, d), F32),
        ],
        compiler_params=pltpu.CompilerParams(
            dimension_semantics=("arbitrary",), vmem_limit_bytes=VMEM_LIMIT_BYTES),
        name="dispatch",
    )(counts, base, n_active_blocks, x1.reshape(n_tok, d), mod_l, norm2_g.reshape(1, d), idx, rank,
      base.reshape(n_exp, 1))


def _expert_kernel(blk_e_sm, first_sm, nact_sm, xs_ref, w1_ref, b1_ref, w2_ref, b2_ref, ys_ref,
                   w1b_ref, w2b_ref):
    g = pl.program_id(0)
    d_ff = w2_ref.shape[0]

    @pl.when(first_sm[g] == 1)
    def _():
        w1b_ref[...] = w1_ref[...].astype(BF16)
        w2b_ref[...] = w2_ref[...].astype(BF16)

    @pl.when(g < nact_sm[0])
    def _():
        hgu = jnp.dot(xs_ref[...].astype(BF16), w1b_ref[...], preferred_element_type=F32) + b1_ref[...]
        gt = jnp.minimum(hgu[:, :d_ff], SWIGLU_LIMIT)
        up = jnp.clip(hgu[:, d_ff:], -SWIGLU_LIMIT, SWIGLU_LIMIT)
        act = (up + 1.0) * (gt * jax.nn.sigmoid(SWIGLU_ALPHA * gt))
        ys_ref[...] = jnp.dot(act.astype(BF16), w2b_ref[...], preferred_element_type=F32) + b2_ref[...]

    @pl.when(g >= nact_sm[0])
    def _():
        ys_ref[...] = jnp.zeros_like(ys_ref)


def _experts(xs, block_e, first, n_active_blocks, layer, w1, b1, w2, b2):
    n_slots, d = xs.shape
    depth, n_exp, _, d_hid = w1.shape
    d_ff = w2.shape[2]
    n_blocks = n_slots // EXPERT_ROWS
    grid_spec = pltpu.PrefetchScalarGridSpec(
        num_scalar_prefetch=3,
        grid=(n_blocks,),
        in_specs=[
            pl.BlockSpec((EXPERT_ROWS, d), lambda g, be, fi, na: (g, 0)),
            pl.BlockSpec((None, None, d, d_hid), lambda g, be, fi, na: (layer, be[g], 0, 0)),
            pl.BlockSpec((None, None, 1, d_hid), lambda g, be, fi, na: (layer, be[g], 0, 0)),
            pl.BlockSpec((None, None, d_ff, d), lambda g, be, fi, na: (layer, be[g], 0, 0)),
            pl.BlockSpec((None, None, 1, d), lambda g, be, fi, na: (layer, be[g], 0, 0)),
        ],
        out_specs=pl.BlockSpec((EXPERT_ROWS, d), lambda g, be, fi, na: (g, 0)),
        scratch_shapes=[pltpu.VMEM((d, d_hid), BF16), pltpu.VMEM((d_ff, d), BF16)],
    )
    return pl.pallas_call(
        _expert_kernel,
        grid_spec=grid_spec,
        out_shape=jax.ShapeDtypeStruct((n_slots, d), F32),
        compiler_params=pltpu.CompilerParams(
            dimension_semantics=("arbitrary",), vmem_limit_bytes=VMEM_LIMIT_BYTES),
        name="experts",
    )(block_e, first, n_active_blocks, xs, w1, b1.reshape(depth, n_exp, 1, d_hid), w2,
      b2.reshape(depth, n_exp, 1, d))


def _combine_kernel(x_ref, mod_ref, slot_ref, slot_nxt_ref, w_ref, fng_ref, ys_ref, o_ref,
                    y_buf, slot_sm, row_sem, sm_sem, *, final_norm):
    rows = x_ref.shape[0]
    step = pl.program_id(0)
    n_steps = pl.num_programs(0)
    par = step % 2

    def row_copy(buf, k, t, slot):
        return pltpu.make_async_copy(ys_ref.at[pl.ds(slot, 1), :], y_buf.at[buf, k, pl.ds(t, 1), :],
                                     row_sem.at[buf])

    def gather(src_ref, buf):
        to_smem = pltpu.make_async_copy(src_ref, slot_sm, sm_sem)
        to_smem.start()
        to_smem.wait()

        def issue(t, c):
            for k in range(TOP_K):
                row_copy(buf, k, t, slot_sm[k, t]).start()
            return c
        lax.fori_loop(0, rows, issue, 0, unroll=8)

    @pl.when(step == 0)
    def _():
        gather(slot_ref, 0)

    @pl.when(step + 1 < n_steps)
    def _():
        gather(slot_nxt_ref, 1 - par)

    for k in range(TOP_K):
        pltpu.make_async_copy(ys_ref.at[pl.ds(0, rows), :], y_buf.at[par, k], row_sem.at[par]).wait()

    w_t = jnp.transpose(jnp.concatenate([w_ref[...], jnp.zeros((LANES - TOP_K, rows), F32)], axis=0))
    moe = w_t[:, 0:1] * y_buf[par, 0]
    for k in range(1, TOP_K):
        moe = moe + w_t[:, k:k + 1] * y_buf[par, k]
    out = x_ref[...] + mod_ref[5:6, :] * moe
    if final_norm:
        out = _rms_normalize(out) * fng_ref[...]
    o_ref[...] = out


def _combine(x1, mod_l, slot, top_w, ys, final_norm_g, final_norm):
    bsz, seq, d = x1.shape
    n_tok = bsz * seq
    rows = ROUTE_ROWS
    tiles_per_seq = seq // rows
    n_steps = n_tok // rows
    tok_spec = pl.BlockSpec((TOP_K, rows), lambda i: (0, i))
    nxt_spec = pl.BlockSpec((TOP_K, rows), lambda i: (0, jnp.minimum(i + 1, n_steps - 1)))
    out = pl.pallas_call(
        functools.partial(_combine_kernel, final_norm=final_norm),
        grid=(n_steps,),
        in_specs=[
            pl.BlockSpec((rows, d), lambda i: (i, 0)),
            pl.BlockSpec((None, N_MOD, d), lambda i: (i // tiles_per_seq, 0, 0)),
            tok_spec,
            nxt_spec,
            tok_spec,
            pl.BlockSpec((1, d), lambda i: (0, 0)),
            pl.BlockSpec(memory_space=pl.ANY),
        ],
        out_specs=pl.BlockSpec((rows, d), lambda i: (i, 0)),
        out_shape=jax.ShapeDtypeStruct((n_tok, d), F32),
        scratch_shapes=[
            pltpu.VMEM((2, TOP_K, rows, d), F32),
            pltpu.SMEM((TOP_K, rows), I32),
            pltpu.SemaphoreType.DMA((2,)),
            pltpu.SemaphoreType.DMA,
        ],
        compiler_params=pltpu.CompilerParams(
            dimension_semantics=("arbitrary",), vmem_limit_bytes=VMEM_LIMIT_BYTES),
        name="combine",
    )(x1.reshape(n_tok, d), mod_l, slot, slot, top_w, final_norm_g.reshape(1, d), ys)
    return out.reshape(bsz, seq, d)


def _moe_layer(x1, mod_l, norm2_g, router_w, router_b, layer, w1, b1, w2, b2, final_norm_g, final_norm):
    bsz, seq, _ = x1.shape
    n_tok = bsz * seq
    n_exp = router_w.shape[1]
    idx, rank, top_w, cnt = _router(x1, mod_l, norm2_g, router_w, router_b)

    counts = cnt[:, 0]
    padded = (counts + EXPERT_ROWS - 1) // EXPERT_ROWS * EXPERT_ROWS
    pad_end = jnp.cumsum(padded)
    base = (pad_end - padded).astype(I32)
    n_blocks = (n_tok * TOP_K) // EXPERT_ROWS + n_exp
    n_active = (pad_end[-1] // EXPERT_ROWS).astype(I32)
    blk_row = jnp.minimum(jnp.arange(n_blocks, dtype=I32), n_active - 1) * EXPERT_ROWS
    block_e = jnp.minimum(jnp.sum(pad_end[None, :] <= blk_row[:, None], axis=1), n_exp - 1).astype(I32)
    is_first = jnp.any(blk_row[:, None] == base[None, :], axis=1)
    first = (is_first & (jnp.arange(n_blocks) < n_active)).astype(I32)
    n_active = n_active.reshape(1)

    slot, xs = _dispatch(x1, mod_l, norm2_g, idx, rank, counts, base, n_active, n_blocks)
    ys = _experts(xs, block_e, first, n_active, layer, w1, b1, w2, b2)
    return _combine(x1, mod_l, slot, top_w, ys, final_norm_g, final_norm)


def kernel(x, c, norm1_g, norm2_g, w_ada, b_ada, w_in, conv_a_w, sgu_w, sgu_b, sgu_ln_g, sgu_ln_b,
           conv_c_w, conv_c_b, rglru_gate_w, rglru_gate_b, rglru_lambda, mix_norm_g, w_out, router_w,
           router_b, expert_w1, expert_b1, expert_w2, expert_b2, final_norm_g):
    depth = w_ada.shape[0]
    mod = _adaln_modulation(c, w_ada, b_ada)
    for l in range(depth):
        x1 = _mixer(x, mod[l], norm1_g[l], w_in[l], conv_a_w[l], sgu_w[l], sgu_b[l], sgu_ln_g[l],
                    sgu_ln_b[l], conv_c_w[l], conv_c_b[l], rglru_gate_w[l], rglru_gate_b[l],
                    rglru_lambda[l], mix_norm_g[l], w_out[l])
        x = _moe_layer(x1, mod[l], norm2_g[l], router_w[l], router_b[l], l, expert_w1, expert_b1,
                       expert_w2, expert_b2, final_norm_g, l == depth - 1)
    return x
```

```python
import functools

import jax
import jax.numpy as jnp
from jax import lax
from jax.experimental import pallas as pl
from jax.experimental.pallas import tpu as pltpu

F32, BF16, I32, U32 = jnp.float32, jnp.bfloat16, jnp.int32, jnp.uint32

HEAD_DIM = 64
CHUNK = 128
RGLRU_C = 8.0
TOP_K = 4
SWIGLU_LIMIT = 7.0
SWIGLU_ALPHA = 1.702
N_MOD = 6
EPS = 1e-6

SUBLANES = 8
LANES = 128
VMEM_LIMIT_BYTES = 56 * 1024 * 1024

MIX_ROWS = 256
ROUTE_ROWS = 256
EXPERT_ROWS = 256
SEG_ALIGN = SUBLANES
MOD_COLS = 1024


def _rms_normalize(y):
    return y * lax.rsqrt(jnp.mean(y * y, axis=-1, keepdims=True) + EPS)


def _round_up(x, m):
    return (x + m - 1) // m * m


def _mod_kernel(c_ref, w_ref, b_ref, o_ref):
    o_ref[...] = (
        jnp.dot(c_ref[...], w_ref[...], preferred_element_type=F32, precision=lax.Precision.HIGHEST)
        + b_ref[...]
    )


def _adaln_modulation(c, w_ada, b_ada):
    depth, d, n = w_ada.shape
    bsz = c.shape[0]
    rows = -(-bsz // SUBLANES) * SUBLANES
    c_pad = jnp.pad(c, ((0, rows - bsz), (0, 0)))
    out = pl.pallas_call(
        _mod_kernel,
        grid=(depth, n // MOD_COLS),
        in_specs=[
            pl.BlockSpec((rows, d), lambda l, j: (0, 0)),
            pl.BlockSpec((None, d, MOD_COLS), lambda l, j: (l, 0, j)),
            pl.BlockSpec((None, 1, MOD_COLS), lambda l, j: (l, 0, j)),
        ],
        out_specs=pl.BlockSpec((None, rows, MOD_COLS), lambda l, j: (l, 0, j)),
        out_shape=jax.ShapeDtypeStruct((depth, rows, n), F32),
        name="adaln_mod",
    )(c_pad, w_ada, b_ada.reshape(depth, 1, n))
    return out[:, :bsz].reshape(depth, bsz, N_MOD, d)


def _mixer_kernel(x_ref, mod_ref, n1g_ref, win_ref, caw_ref, sguw_ref, sgub_ref, lng_ref, lnb_ref,
                  ccw_ref, ccb_ref, gw_ref, gb_ref, lam_ref, mng_ref, wout_ref,
                  o_ref, cvh_ref, zxh_ref, hc_ref, a_scr, b_scr, *, w_conv, w_sgu, w_lru):
    rows = x_ref.shape[0]
    hist = SUBLANES

    @pl.when(pl.program_id(1) == 0)
    def _():
        cvh_ref[0:hist, :] = jnp.zeros((hist, w_conv), F32)
        zxh_ref[0:hist, :] = jnp.zeros((hist, w_lru), F32)
        hc_ref[...] = jnp.zeros_like(hc_ref)

    x = x_ref[...]
    shift1, scale1, gate1 = mod_ref[0:1, :], mod_ref[1:2, :], mod_ref[2:3, :]
    h = _rms_normalize(x) * n1g_ref[...]
    h = h * (1.0 + scale1) + shift1
    z = jnp.dot(h.astype(BF16), win_ref[...], preferred_element_type=F32)

    o = 0
    z_b = z[:, o:o + w_conv]; o += w_conv
    z_c = z[:, o:o + w_conv]; o += w_conv
    z_v = z[:, o:o + w_conv]; o += w_conv
    z_su = z[:, o:o + w_sgu]; o += w_sgu
    z_sv = z[:, o:o + w_sgu]; o += w_sgu
    z_g = z[:, o:o + w_lru]; o += w_lru
    z_x = z[:, o:o + w_lru]

    cv = z_c * z_v
    cvh_ref[hist:hist + rows, :] = cv
    conv = (caw_ref[0:1, :] * cvh_ref[hist - 2:hist - 2 + rows, :]
            + caw_ref[1:2, :] * cvh_ref[hist - 1:hist - 1 + rows, :]
            + caw_ref[2:3, :] * cv)
    cvh_ref[0:hist, :] = cvh_ref[rows:rows + hist, :]
    y_conv = z_b * conv

    u = jax.nn.gelu(z_su)
    v = jax.nn.gelu(z_sv)
    vc = v - jnp.mean(v, axis=-1, keepdims=True)
    vn = vc * lax.rsqrt(jnp.mean(vc * vc, axis=-1, keepdims=True) + EPS)
    vn = (vn * lng_ref[...] + lnb_ref[...]).astype(BF16)
    n_heads = w_sgu // HEAD_DIM
    wr = lax.broadcasted_iota(I32, (n_heads * CHUNK, CHUNK), 0)
    wc = lax.broadcasted_iota(I32, (n_heads * CHUNK, CHUNK), 1)
    w_causal = jnp.where(wc <= (wr & (CHUNK - 1)), sguw_ref[...], 0.0).astype(BF16)
    lane = lax.broadcasted_iota(I32, (CHUNK, w_sgu), 1)
    mixed = []
    for ci in range(rows // CHUNK):
        res = jnp.dot(w_causal, vn[ci * CHUNK:(ci + 1) * CHUNK, :], preferred_element_type=F32)
        m = res[0:CHUNK, :]
        for hh in range(1, n_heads):
            m = jnp.where(lane >= hh * HEAD_DIM, res[hh * CHUNK:(hh + 1) * CHUNK, :], m)
        mixed.append(m + sgub_ref[...])
    y_sgu = u * jnp.concatenate(mixed, axis=0)

    zxh_ref[hist:hist + rows, :] = z_x
    xr = (ccw_ref[0:1, :] * zxh_ref[hist - 3:hist - 3 + rows, :]
          + ccw_ref[1:2, :] * zxh_ref[hist - 2:hist - 2 + rows, :]
          + ccw_ref[2:3, :] * zxh_ref[hist - 1:hist - 1 + rows, :]
          + ccw_ref[3:4, :] * z_x + ccb_ref[...])
    zxh_ref[0:hist, :] = zxh_ref[rows:rows + hist, :]
    xr_b = xr.astype(BF16)
    half = w_lru // 2
    g0 = jnp.dot(xr_b[:, :half], gw_ref[0], preferred_element_type=F32)
    g1 = jnp.dot(xr_b[:, half:], gw_ref[1], preferred_element_type=F32)
    r_gate = jax.nn.sigmoid(jnp.concatenate([g0[:, :half], g1[:, :half]], axis=-1) + gb_ref[0:1, :])
    i_gate = jax.nn.sigmoid(jnp.concatenate([g0[:, half:], g1[:, half:]], axis=-1) + gb_ref[1:2, :])
    neg_lam = -lam_ref[...]
    softplus = jnp.maximum(neg_lam, 0.0) + jnp.log1p(jnp.exp(-jnp.abs(neg_lam)))
    log_a = (-RGLRU_C) * r_gate * softplus
    a = jnp.exp(log_a)
    a_scr[...] = a
    b_scr[...] = xr * i_gate * jnp.sqrt(1.0 - a * a)

    row8 = lax.broadcasted_iota(I32, (SUBLANES, w_lru), 0)

    def scan_group(g, carry):
        r0 = pl.multiple_of(g * SUBLANES, SUBLANES)
        a8 = a_scr[pl.ds(r0, SUBLANES), :]
        b8 = b_scr[pl.ds(r0, SUBLANES), :]
        for d in (1, 2, 4):
            keep = row8 >= d
            b8 = jnp.where(keep, a8 * pltpu.roll(b8, d, 0) + b8, b8)
            a8 = jnp.where(keep, a8 * pltpu.roll(a8, d, 0), a8)
        h8 = a8 * carry + b8
        b_scr[pl.ds(r0, SUBLANES), :] = h8
        return jnp.broadcast_to(h8[SUBLANES - 1:SUBLANES, :], (SUBLANES, w_lru))

    hc_ref[...] = lax.fori_loop(0, rows // SUBLANES, scan_group, hc_ref[...], unroll=4)
    y_lru = jax.nn.gelu(z_g) * b_scr[...]

    y = jnp.concatenate([_rms_normalize(y_conv), _rms_normalize(y_sgu), _rms_normalize(y_lru)], axis=-1)
    y = (y * mng_ref[...]).astype(BF16)
    o_ref[...] = x + gate1 * jnp.dot(y, wout_ref[...], preferred_element_type=F32)


def _mixer(x, mod_l, norm1_g, w_in, conv_a_w, sgu_w, sgu_b, sgu_ln_g, sgu_ln_b, conv_c_w, conv_c_b,
           rglru_gate_w, rglru_gate_b, rglru_lambda, mix_norm_g, w_out):
    bsz, seq, d = x.shape
    w_conv = conv_a_w.shape[1]
    w_sgu = sgu_ln_g.shape[0]
    w_lru = rglru_lambda.shape[0]
    d_in = w_in.shape[1]
    n_sgu_heads = sgu_w.shape[0]
    n_lru_heads = rglru_gate_w.shape[1]
    rows = MIX_ROWS
    half = w_lru // 2

    sgu_bias = jnp.repeat(sgu_b.T, HEAD_DIM, axis=1)
    eye = jnp.eye(n_lru_heads, dtype=F32)
    bd = jnp.einsum("ghij,hk->ghikj", rglru_gate_w, eye).reshape(2, w_lru, w_lru)
    gate_w = jnp.stack([
        jnp.concatenate([bd[0, j * half:(j + 1) * half, j * half:(j + 1) * half],
                         bd[1, j * half:(j + 1) * half, j * half:(j + 1) * half]], axis=1)
        for j in range(2)]).astype(BF16)

    full = lambda shape: pl.BlockSpec(shape, lambda b, s: (0,) * len(shape))
    kern = functools.partial(_mixer_kernel, w_conv=w_conv, w_sgu=w_sgu, w_lru=w_lru)
    return pl.pallas_call(
        kern,
        grid=(bsz, seq // rows),
        in_specs=[
            pl.BlockSpec((None, rows, d), lambda b, s: (b, s, 0)),
            pl.BlockSpec((None, N_MOD, d), lambda b, s: (b, 0, 0)),
            full((1, d)),
            full((d, d_in)),
            full(conv_a_w.shape),
            full((n_sgu_heads * CHUNK, CHUNK)),
            full((CHUNK, w_sgu)),
            full((1, w_sgu)),
            full((1, w_sgu)),
            full(conv_c_w.shape),
            full((1, w_lru)),
            full((2, half, 2 * half)),
            full((2, w_lru)),
            full((1, w_lru)),
            full((1, d)),
            full((d, d)),
        ],
        out_specs=pl.BlockSpec((None, rows, d), lambda b, s: (b, s, 0)),
        out_shape=jax.ShapeDtypeStruct((bsz, seq, d), F32),
        scratch_shapes=[
            pltpu.VMEM((rows + SUBLANES, w_conv), F32),
            pltpu.VMEM((rows + SUBLANES, w_lru), F32),
            pltpu.VMEM((SUBLANES, w_lru), F32),
            pltpu.VMEM((rows, w_lru), F32),
            pltpu.VMEM((rows, w_lru), F32),
        ],
        compiler_params=pltpu.CompilerParams(
            dimension_semantics=("arbitrary", "arbitrary"), vmem_limit_bytes=VMEM_LIMIT_BYTES),
        name="mixer",
    )(x, mod_l, norm1_g.reshape(1, d), w_in.astype(BF16), conv_a_w,
      sgu_w.reshape(n_sgu_heads * CHUNK, CHUNK), sgu_bias, sgu_ln_g.reshape(1, w_sgu),
      sgu_ln_b.reshape(1, w_sgu), conv_c_w, conv_c_b.reshape(1, w_lru), gate_w,
      rglru_gate_b.reshape(2, w_lru), rglru_lambda.reshape(1, w_lru), mix_norm_g.reshape(1, d),
      w_out.astype(BF16))


def _pack_bf16_pairs(a, b):
    ua = lax.bitcast_convert_type(a, U32)
    ub = lax.bitcast_convert_type(b, U32)
    return (ua & jnp.uint32(0xFFFF0000)) | (ub >> 16)


def _unpack_bf16_pairs(p):
    a = lax.bitcast_convert_type(p & jnp.uint32(0xFFFF0000), F32)
    b = lax.bitcast_convert_type(p << 16, F32)
    return a, b


def _stage_rows(n_exp):
    return _round_up(TOP_K * ROUTE_ROWS + n_exp * (SEG_ALIGN - 1), ROUTE_ROWS)


def _modulated_norm(x, g, scale, shift):
    return (_rms_normalize(x) * g) * (1.0 + scale) + shift


def _router_kernel(x_ref, mod_ref, n2g_ref, rwt_ref, rb_ref, xs_ref, pos_ref, cnt_ref):
    rows, d = x_ref.shape
    n_exp = rwt_ref.shape[0]
    stage = xs_ref.shape[0]
    half = d // 2

    h2 = _modulated_norm(x_ref[...], n2g_ref[...], mod_ref[4:5, :], mod_ref[3:4, :])
    logits = lax.dot_general(rwt_ref[...], h2, (((1,), (1,)), ((), ())),
                             preferred_element_type=F32, precision=lax.Precision.HIGHEST)
    logits = logits + rb_ref[...]

    e_iota = lax.broadcasted_iota(I32, (n_exp, rows), 0).astype(F32)
    work = logits
    tops, hots = [], []
    for k in range(TOP_K):
        m = jnp.max(work, axis=0, keepdims=True)
        idx = jnp.min(jnp.where(work == m, e_iota, float(n_exp)), axis=0, keepdims=True)
        hot = e_iota == idx
        work = jnp.where(hot, -jnp.inf, work)
        tops.append(m)
        hots.append(hot)

    p = [jnp.exp(t - tops[0]) for t in tops]
    denom = p[0] + p[1] + p[2] + p[3]

    sel = jnp.zeros((n_exp, rows), F32)
    for hot in hots:
        sel = sel + hot.astype(F32)
    s_io = lax.broadcasted_iota(I32, (rows, rows), 0)
    t_io = lax.broadcasted_iota(I32, (rows, rows), 1)
    before = (s_io < t_io).astype(BF16)
    rank_e = jnp.dot(sel.astype(BF16), before, preferred_element_type=F32)
    n_e = jnp.sum(sel, axis=1, keepdims=True)
    seg_len = jnp.floor((n_e + (SEG_ALIGN - 1)) * (1.0 / SEG_ALIGN)) * SEG_ALIGN
    incl = jnp.broadcast_to(seg_len, (n_exp, LANES))
    e_row = lax.broadcasted_iota(I32, (n_exp, LANES), 0)
    step = 1
    while step < n_exp:
        incl = incl + jnp.where(e_row >= step, pltpu.roll(incl, step, 0), 0.0)
        step *= 2
    dest_e = (incl[:, 0:1] - seg_len) + rank_e

    r_iota = lax.broadcasted_iota(I32, (stage, rows), 0).astype(F32)
    onehot = None
    wmat = jnp.zeros((stage, rows), F32)
    for k in range(TOP_K):
        pos_k = jnp.sum(jnp.where(hots[k], dest_e, 0.0), axis=0, keepdims=True)
        pos_ref[k:k + 1, :] = pos_k.astype(I32)
        eq = r_iota == pos_k
        onehot = eq if onehot is None else (onehot | eq)
        wmat = jnp.where(eq, p[k] / denom, wmat)
    perm = jnp.where(onehot, 1.0, 0.0).astype(BF16)
    staged = jnp.dot(perm, h2.astype(BF16), preferred_element_type=F32)
    wcol = jnp.sum(wmat, axis=1, keepdims=True)
    xs_ref[:, 0:half] = _pack_bf16_pairs(staged[:, :half], staged[:, half:])
    xs_ref[:, half:] = lax.bitcast_convert_type(jnp.broadcast_to(wcol, (stage, LANES)), U32)
    cnt_ref[...] = jnp.broadcast_to(n_e, (n_exp, LANES)).astype(I32)


def _router(x1, mod_l, norm2_g, router_w, router_b):
    bsz, seq, d = x1.shape
    n_tok = bsz * seq
    n_exp = router_w.shape[1]
    rows = ROUTE_ROWS
    n_win = n_tok // rows
    tiles_per_seq = seq // rows
    stage = _stage_rows(n_exp)
    return pl.pallas_call(
        _router_kernel,
        grid=(n_win,),
        in_specs=[
            pl.BlockSpec((rows, d), lambda i: (i, 0)),
            pl.BlockSpec((None, N_MOD, d), lambda i: (i // tiles_per_seq, 0, 0)),
            pl.BlockSpec((1, d), lambda i: (0, 0)),
            pl.BlockSpec((n_exp, d), lambda i: (0, 0)),
            pl.BlockSpec((n_exp, 1), lambda i: (0, 0)),
        ],
        out_specs=[
            pl.BlockSpec((None, stage, d // 2 + LANES), lambda i: (i, 0, 0)),
            pl.BlockSpec((TOP_K, rows), lambda i: (0, i)),
            pl.BlockSpec((None, n_exp, LANES), lambda i: (i, 0, 0)),
        ],
        out_shape=[
            jax.ShapeDtypeStruct((n_win, stage, d // 2 + LANES), U32),
            jax.ShapeDtypeStruct((TOP_K, n_tok), I32),
            jax.ShapeDtypeStruct((n_win, n_exp, LANES), I32),
        ],
        compiler_params=pltpu.CompilerParams(
            dimension_semantics=("arbitrary",), vmem_limit_bytes=VMEM_LIMIT_BYTES),
        name="router",
    )(x1.reshape(n_tok, d), mod_l, norm2_g.reshape(1, d), router_w.T, router_b.reshape(n_exp, 1))


def _expert_kernel(be_sm, first_sm, nact_sm, lo_sm, ws_sm, we_sm, valid_sm, roff_sm, len_sm, src_sm,
                   used_sm, xs_ref, w1_ref, b1_ref, w2_ref, b2_ref, ys_ref,
                   w1b_ref, w2b_ref, xbuf, ybuf, zbuf, in_sem, out_sem, tail_sem, *, n_exp, n_win):
    g = pl.program_id(0)
    n_steps = pl.num_programs(0)
    nact = nact_sm[0]
    d_ff = w2_ref.shape[0]
    half = ybuf.shape[2]
    stage = ys_ref.shape[1]

    def pieces(blk, fn):
        e = be_sm[blk]
        lo = lo_sm[blk]

        def body(w, c):
            i = w * n_exp + e
            seg_lo = roff_sm[i]
            a = jnp.maximum(seg_lo, lo)
            b = jnp.minimum(seg_lo + len_sm[i], lo + EXPERT_ROWS)

            @pl.when(b > a)
            def _():
                fn(w, pl.multiple_of(src_sm[i] + (a - seg_lo), SEG_ALIGN),
                   pl.multiple_of(a - lo, SEG_ALIGN), pl.multiple_of(b - a, SEG_ALIGN))
            return c
        lax.fori_loop(ws_sm[blk], we_sm[blk], body, 0)

    def in_copy(buf):
        return lambda w, src, dst, n: pltpu.make_async_copy(
            xs_ref.at[w, pl.ds(src, n), :], xbuf.at[buf, pl.ds(dst, n), :], in_sem.at[buf])

    def out_copy(buf):
        return lambda w, src, dst, n: pltpu.make_async_copy(
            ybuf.at[buf, pl.ds(dst, n), :], ys_ref.at[w, pl.ds(src, n), :], out_sem.at[buf])

    def tail_copy(w):
        used = pl.multiple_of(used_sm[w], SEG_ALIGN)
        n = pl.multiple_of(stage - used, SEG_ALIGN)
        return pltpu.make_async_copy(zbuf.at[pl.ds(0, n), :], ys_ref.at[w, pl.ds(used, n), :], tail_sem)

    def for_windows(fn):
        def body(w, c):
            fn(w)
            return c
        lax.fori_loop(0, n_win, body, 0)

    @pl.when(g == 0)
    def _():
        xbuf[...] = jnp.zeros_like(xbuf)
        zbuf[...] = jnp.zeros_like(zbuf)
        for_windows(lambda w: tail_copy(w).start())
        pieces(0, lambda *a: in_copy(0)(*a).start())

    @pl.when(g + 1 < nact)
    def _():
        pieces(g + 1, lambda *a: in_copy((g + 1) % 2)(*a).start())

    @pl.when((g >= 2) & (g - 2 < nact))
    def _():
        pieces(g - 2, lambda *a: out_copy(g % 2)(*a).wait())

    @pl.when(first_sm[g] == 1)
    def _():
        w1b_ref[...] = w1_ref[...].astype(BF16)
        w2b_ref[...] = w2_ref[...].astype(BF16)

    @pl.when(g < nact)
    def _():
        buf = g % 2
        pieces(g, lambda *a: in_copy(buf)(*a).wait())
        packed = xbuf[buf]
        keep = lax.broadcasted_iota(I32, (EXPERT_ROWS, 1), 0) < valid_sm[g]
        xa, xb = _unpack_bf16_pairs(packed[:, :half])
        x = jnp.where(keep, jnp.concatenate([xa, xb], axis=-1), 0.0).astype(BF16)
        wcol = jnp.where(keep, lax.bitcast_convert_type(packed[:, half:half + 1], F32), 0.0)
        hgu = jnp.dot(x, w1b_ref[...], preferred_element_type=F32) + b1_ref[...]
        gt = jnp.minimum(hgu[:, :d_ff], SWIGLU_LIMIT)
        up = jnp.clip(hgu[:, d_ff:], -SWIGLU_LIMIT, SWIGLU_LIMIT)
        act = (up + 1.0) * (gt * jax.nn.sigmoid(SWIGLU_ALPHA * gt))
        y = (jnp.dot(act.astype(BF16), w2b_ref[...], preferred_element_type=F32) + b2_ref[...]) * wcol
        y = y.astype(BF16).astype(F32)
        ybuf[buf] = _pack_bf16_pairs(y[:, :half], y[:, half:])
        pieces(g, lambda *a: out_copy(buf)(*a).start())

    @pl.when(g == n_steps - 1)
    def _():
        for back in (1, 0):
            blk = g - back

            @pl.when((blk >= 0) & (blk < nact))
            def _():
                pieces(blk, lambda *a: out_copy(blk % 2)(*a).wait())
        for_windows(lambda w: tail_copy(w).wait())


def _experts(xs, tables, layer, w1, b1, w2, b2):
    n_win, stage, xw = xs.shape
    depth, n_exp, d, d_hid = w1.shape
    d_ff = w2.shape[2]
    half = d // 2
    n_blocks = tables[0].shape[0]
    wmap = lambda g, be, *_: (layer, be[g], 0, 0)
    grid_spec = pltpu.PrefetchScalarGridSpec(
        num_scalar_prefetch=len(tables),
        grid=(n_blocks,),
        in_specs=[
            pl.BlockSpec(memory_space=pl.ANY),
            pl.BlockSpec((None, None, d, d_hid), wmap),
            pl.BlockSpec((None, None, 1, d_hid), wmap),
            pl.BlockSpec((None, None, d_ff, d), wmap),
            pl.BlockSpec((None, None, 1, d), wmap),
        ],
        out_specs=pl.BlockSpec(memory_space=pl.ANY),
        scratch_shapes=[
            pltpu.VMEM((d, d_hid), BF16),
            pltpu.VMEM((d_ff, d), BF16),
            pltpu.VMEM((2, EXPERT_ROWS, xw), U32),
            pltpu.VMEM((2, EXPERT_ROWS, half), U32),
            pltpu.VMEM((stage - TOP_K * ROUTE_ROWS, half), U32),
            pltpu.SemaphoreType.DMA((2,)),
            pltpu.SemaphoreType.DMA((2,)),
            pltpu.SemaphoreType.DMA,
        ],
    )
    return pl.pallas_call(
        functools.partial(_expert_kernel, n_exp=n_exp, n_win=n_win),
        grid_spec=grid_spec,
        out_shape=jax.ShapeDtypeStruct((n_win, stage, half), U32),
        compiler_params=pltpu.CompilerParams(
            dimension_semantics=("arbitrary",), vmem_limit_bytes=VMEM_LIMIT_BYTES),
        name="experts",
    )(*tables, xs, w1, b1.reshape(depth, n_exp, 1, d_hid), w2, b2.reshape(depth, n_exp, 1, d))


def _combine_kernel(x_ref, mod_ref, pos_ref, fng_ref, ys_ref, o_ref, *, final_norm):
    rows = x_ref.shape[0]
    stage = ys_ref.shape[0]
    pos_rows = jnp.concatenate(
        [pos_ref[...].astype(F32), jnp.full((LANES - TOP_K, rows), -1.0, F32)], axis=0)
    pos_t = jnp.transpose(pos_rows)
    c_iota = lax.broadcasted_iota(I32, (rows, stage), 1).astype(F32)
    hit = c_iota == pos_t[:, 0:1]
    for k in range(1, TOP_K):
        hit = hit | (c_iota == pos_t[:, k:k + 1])
    unperm = jnp.where(hit, 1.0, 0.0).astype(BF16)
    ya, yb = _unpack_bf16_pairs(ys_ref[...])
    moe = jnp.concatenate(
        [jnp.dot(unperm, ya.astype(BF16), preferred_element_type=F32),
         jnp.dot(unperm, yb.astype(BF16), preferred_element_type=F32)], axis=-1)
    out = x_ref[...] + mod_ref[5:6, :] * moe
    if final_norm:
        out = _rms_normalize(out) * fng_ref[...]
    o_ref[...] = out


def _combine(x1, mod_l, pos, ys, final_norm_g, final_norm):
    bsz, seq, d = x1.shape
    n_tok = bsz * seq
    rows = ROUTE_ROWS
    tiles_per_seq = seq // rows
    n_win, stage, half = ys.shape
    out = pl.pallas_call(
        functools.partial(_combine_kernel, final_norm=final_norm),
        grid=(n_win,),
        in_specs=[
            pl.BlockSpec((rows, d), lambda i: (i, 0)),
            pl.BlockSpec((None, N_MOD, d), lambda i: (i // tiles_per_seq, 0, 0)),
            pl.BlockSpec((TOP_K, rows), lambda i: (0, i)),
            pl.BlockSpec((1, d), lambda i: (0, 0)),
            pl.BlockSpec((None, stage, half), lambda i: (i, 0, 0)),
        ],
        out_specs=pl.BlockSpec((rows, d), lambda i: (i, 0)),
        out_shape=jax.ShapeDtypeStruct((n_tok, d), F32),
        compiler_params=pltpu.CompilerParams(
            dimension_semantics=("arbitrary",), vmem_limit_bytes=VMEM_LIMIT_BYTES),
        name="combine",
    )(x1.reshape(n_tok, d), mod_l, pos, final_norm_g.reshape(1, d), ys)
    return out.reshape(bsz, seq, d)


def _segment_tables(cnt, stage):
    n_win, n_exp = cnt.shape
    seg_len = _round_up(cnt, SEG_ALIGN)
    seg_src = jnp.cumsum(seg_len, axis=1) - seg_len
    used = jnp.sum(seg_len, axis=1)
    seg_off = jnp.cumsum(seg_len, axis=0) - seg_len
    reg_len = jnp.sum(seg_len, axis=0)
    reg_pad = _round_up(reg_len, EXPERT_ROWS)
    pad_end = jnp.cumsum(reg_pad)
    base = pad_end - reg_pad
    max_rows = n_win * (TOP_K * ROUTE_ROWS + n_exp * (SEG_ALIGN - 1)) + n_exp * (EXPERT_ROWS - SEG_ALIGN)
    n_blocks = -(-max_rows // EXPERT_ROWS)
    n_active = pad_end[-1] // EXPERT_ROWS
    gidx = jnp.arange(n_blocks, dtype=I32)
    blk_row = jnp.minimum(gidx, n_active - 1) * EXPERT_ROWS
    block_e = jnp.minimum(jnp.sum(pad_end[None, :] <= blk_row[:, None], axis=1), n_exp - 1)
    blk_lo = blk_row - base[block_e]
    first = (blk_lo == 0) & (gidx < n_active)
    lo_g = seg_off.T[block_e]
    hi_g = lo_g + seg_len.T[block_e]
    w_start = jnp.sum(hi_g <= blk_lo[:, None], axis=1)
    w_end = jnp.sum(lo_g < (blk_lo + EXPERT_ROWS)[:, None], axis=1)
    valid = jnp.clip(reg_len[block_e] - blk_lo, 0, EXPERT_ROWS)
    del stage
    tables = (block_e, first, n_active.reshape(1), blk_lo, w_start, w_end, valid,
              seg_off.reshape(-1), seg_len.reshape(-1), seg_src.reshape(-1), used)
    return tuple(t.astype(I32) for t in tables)


def _moe_layer(x1, mod_l, norm2_g, router_w, router_b, layer, w1, b1, w2, b2, final_norm_g, final_norm):
    xs, pos, cnt = _router(x1, mod_l, norm2_g, router_w, router_b)
    tables = _segment_tables(cnt[:, :, 0], xs.shape[1])
    ys = _experts(xs, tables, layer, w1, b1, w2, b2)
    return _combine(x1, mod_l, pos, ys, final_norm_g, final_norm)


def kernel(x, c, norm1_g, norm2_g, w_ada, b_ada, w_in, conv_a_w, sgu_w, sgu_b, sgu_ln_g, sgu_ln_b,
           conv_c_w, conv_c_b, rglru_gate_w, rglru_gate_b, rglru_lambda, mix_norm_g, w_out, router_w,
           router_b, expert_w1, expert_b1, expert_w2, expert_b2, final_norm_g):
    depth = w_ada.shape[0]
    mod = _adaln_modulation(c, w_ada, b_ada)
    for l in range(depth):
        x1 = _mixer(x, mod[l], norm1_g[l], w_in[l], conv_a_w[l], sgu_w[l], sgu_b[l], sgu_ln_g[l],
                    sgu_ln_b[l], conv_c_w[l], conv_c_b[l], rglru_gate_w[l], rglru_gate_b[l],
                    rglru_lambda[l], mix_norm_g[l], w_out[l])
        x = _moe_layer(x1, mod[l], norm2_g[l], router_w[l], router_b[l], l, expert_w1, expert_b1,
                       expert_w2, expert_b2, final_norm_g, l == depth - 1)
    return x
```

```python
import functools

import jax
import jax.numpy as jnp
from jax import lax
from jax.experimental import pallas as pl
from jax.experimental.pallas import tpu as pltpu

F32, BF16, I32, U32 = jnp.float32, jnp.bfloat16, jnp.int32, jnp.uint32

HEAD_DIM = 64
CHUNK = 128
RGLRU_C = 8.0
TOP_K = 4
SWIGLU_LIMIT = 7.0
SWIGLU_ALPHA = 1.702
N_MOD = 6
EPS = 1e-6

SUBLANES = 8
LANES = 128
VMEM_LIMIT_BYTES = 56 * 1024 * 1024

MIX_ROWS = 256
ROUTE_ROWS = 256
ROUTE_STEP_WINDOWS = 2
PIECE_UNROLL = 4
EXPERT_ROWS = 256
SEG_ALIGN = SUBLANES
MOD_COLS = 1024


def _rms_normalize(y):
    return y * lax.rsqrt(jnp.mean(y * y, axis=-1, keepdims=True) + EPS)


def _round_up(x, m):
    return (x + m - 1) // m * m


def _mod_kernel(c_ref, w_ref, b_ref, o_ref):
    o_ref[...] = (
        jnp.dot(c_ref[...], w_ref[...], preferred_element_type=F32, precision=lax.Precision.HIGHEST)
        + b_ref[...]
    )


def _adaln_modulation(c, w_ada, b_ada):
    depth, d, n = w_ada.shape
    bsz = c.shape[0]
    rows = -(-bsz // SUBLANES) * SUBLANES
    c_pad = jnp.pad(c, ((0, rows - bsz), (0, 0)))
    out = pl.pallas_call(
        _mod_kernel,
        grid=(depth, n // MOD_COLS),
        in_specs=[
            pl.BlockSpec((rows, d), lambda l, j: (0, 0)),
            pl.BlockSpec((None, d, MOD_COLS), lambda l, j: (l, 0, j)),
            pl.BlockSpec((None, 1, MOD_COLS), lambda l, j: (l, 0, j)),
        ],
        out_specs=pl.BlockSpec((None, rows, MOD_COLS), lambda l, j: (l, 0, j)),
        out_shape=jax.ShapeDtypeStruct((depth, rows, n), F32),
        name="adaln_mod",
    )(c_pad, w_ada, b_ada.reshape(depth, 1, n))
    return out[:, :bsz].reshape(depth, bsz, N_MOD, d)


def _mixer_kernel(x_ref, mod_ref, n1g_ref, win_ref, caw_ref, sguw_ref, sgub_ref, lng_ref, lnb_ref,
                  ccw_ref, ccb_ref, gw_ref, gb_ref, lam_ref, mng_ref, wout_ref,
                  o_ref, cvh_ref, zxh_ref, hc_ref, a_scr, b_scr, *, w_conv, w_sgu, w_lru):
    rows = x_ref.shape[0]
    hist = SUBLANES

    @pl.when(pl.program_id(1) == 0)
    def _():
        cvh_ref[0:hist, :] = jnp.zeros((hist, w_conv), F32)
        zxh_ref[0:hist, :] = jnp.zeros((hist, w_lru), F32)
        hc_ref[...] = jnp.zeros_like(hc_ref)

    x = x_ref[...]
    shift1, scale1, gate1 = mod_ref[0:1, :], mod_ref[1:2, :], mod_ref[2:3, :]
    h = _rms_normalize(x) * n1g_ref[...]
    h = h * (1.0 + scale1) + shift1
    z = jnp.dot(h.astype(BF16), win_ref[...], preferred_element_type=F32)

    o = 0
    z_b = z[:, o:o + w_conv]; o += w_conv
    z_c = z[:, o:o + w_conv]; o += w_conv
    z_v = z[:, o:o + w_conv]; o += w_conv
    z_su = z[:, o:o + w_sgu]; o += w_sgu
    z_sv = z[:, o:o + w_sgu]; o += w_sgu
    z_g = z[:, o:o + w_lru]; o += w_lru
    z_x = z[:, o:o + w_lru]

    cv = z_c * z_v
    cvh_ref[hist:hist + rows, :] = cv
    conv = (caw_ref[0:1, :] * cvh_ref[hist - 2:hist - 2 + rows, :]
            + caw_ref[1:2, :] * cvh_ref[hist - 1:hist - 1 + rows, :]
            + caw_ref[2:3, :] * cv)
    cvh_ref[0:hist, :] = cvh_ref[rows:rows + hist, :]
    y_conv = z_b * conv

    u = jax.nn.gelu(z_su)
    v = jax.nn.gelu(z_sv)
    vc = v - jnp.mean(v, axis=-1, keepdims=True)
    vn = vc * lax.rsqrt(jnp.mean(vc * vc, axis=-1, keepdims=True) + EPS)
    vn = (vn * lng_ref[...] + lnb_ref[...]).astype(BF16)
    n_heads = w_sgu // HEAD_DIM
    wr = lax.broadcasted_iota(I32, (n_heads * CHUNK, CHUNK), 0)
    wc = lax.broadcasted_iota(I32, (n_heads * CHUNK, CHUNK), 1)
    w_causal = jnp.where(wc <= (wr & (CHUNK - 1)), sguw_ref[...], 0.0).astype(BF16)
    lane = lax.broadcasted_iota(I32, (CHUNK, w_sgu), 1)
    mixed = []
    for ci in range(rows // CHUNK):
        res = jnp.dot(w_causal, vn[ci * CHUNK:(ci + 1) * CHUNK, :], preferred_element_type=F32)
        m = res[0:CHUNK, :]
        for hh in range(1, n_heads):
            m = jnp.where(lane >= hh * HEAD_DIM, res[hh * CHUNK:(hh + 1) * CHUNK, :], m)
        mixed.append(m + sgub_ref[...])
    y_sgu = u * jnp.concatenate(mixed, axis=0)

    zxh_ref[hist:hist + rows, :] = z_x
    xr = (ccw_ref[0:1, :] * zxh_ref[hist - 3:hist - 3 + rows, :]
          + ccw_ref[1:2, :] * zxh_ref[hist - 2:hist - 2 + rows, :]
          + ccw_ref[2:3, :] * zxh_ref[hist - 1:hist - 1 + rows, :]
          + ccw_ref[3:4, :] * z_x + ccb_ref[...])
    zxh_ref[0:hist, :] = zxh_ref[rows:rows + hist, :]
    xr_b = xr.astype(BF16)
    half = w_lru // 2
    g0 = jnp.dot(xr_b[:, :half], gw_ref[0], preferred_element_type=F32)
    g1 = jnp.dot(xr_b[:, half:], gw_ref[1], preferred_element_type=F32)
    r_gate = jax.nn.sigmoid(jnp.concatenate([g0[:, :half], g1[:, :half]], axis=-1) + gb_ref[0:1, :])
    i_gate = jax.nn.sigmoid(jnp.concatenate([g0[:, half:], g1[:, half:]], axis=-1) + gb_ref[1:2, :])
    neg_lam = -lam_ref[...]
    softplus = jnp.maximum(neg_lam, 0.0) + jnp.log1p(jnp.exp(-jnp.abs(neg_lam)))
    log_a = (-RGLRU_C) * r_gate * softplus
    a = jnp.exp(log_a)
    a_scr[...] = a
    b_scr[...] = xr * i_gate * jnp.sqrt(1.0 - a * a)

    row8 = lax.broadcasted_iota(I32, (SUBLANES, w_lru), 0)

    def scan_group(g, carry):
        r0 = pl.multiple_of(g * SUBLANES, SUBLANES)
        a8 = a_scr[pl.ds(r0, SUBLANES), :]
        b8 = b_scr[pl.ds(r0, SUBLANES), :]
        for d in (1, 2, 4):
            keep = row8 >= d
            b8 = jnp.where(keep, a8 * pltpu.roll(b8, d, 0) + b8, b8)
            a8 = jnp.where(keep, a8 * pltpu.roll(a8, d, 0), a8)
        h8 = a8 * carry + b8
        b_scr[pl.ds(r0, SUBLANES), :] = h8
        return jnp.broadcast_to(h8[SUBLANES - 1:SUBLANES, :], (SUBLANES, w_lru))

    hc_ref[...] = lax.fori_loop(0, rows // SUBLANES, scan_group, hc_ref[...], unroll=4)
    y_lru = jax.nn.gelu(z_g) * b_scr[...]

    y = jnp.concatenate([_rms_normalize(y_conv), _rms_normalize(y_sgu), _rms_normalize(y_lru)], axis=-1)
    y = (y * mng_ref[...]).astype(BF16)
    o_ref[...] = x + gate1 * jnp.dot(y, wout_ref[...], preferred_element_type=F32)


def _mixer(x, mod_l, norm1_g, w_in, conv_a_w, sgu_w, sgu_b, sgu_ln_g, sgu_ln_b, conv_c_w, conv_c_b,
           rglru_gate_w, rglru_gate_b, rglru_lambda, mix_norm_g, w_out):
    bsz, seq, d = x.shape
    w_conv = conv_a_w.shape[1]
    w_sgu = sgu_ln_g.shape[0]
    w_lru = rglru_lambda.shape[0]
    d_in = w_in.shape[1]
    n_sgu_heads = sgu_w.shape[0]
    n_lru_heads = rglru_gate_w.shape[1]
    rows = MIX_ROWS
    half = w_lru // 2

    sgu_bias = jnp.repeat(sgu_b.T, HEAD_DIM, axis=1)
    eye = jnp.eye(n_lru_heads, dtype=F32)
    bd = jnp.einsum("ghij,hk->ghikj", rglru_gate_w, eye).reshape(2, w_lru, w_lru)
    gate_w = jnp.stack([
        jnp.concatenate([bd[0, j * half:(j + 1) * half, j * half:(j + 1) * half],
                         bd[1, j * half:(j + 1) * half, j * half:(j + 1) * half]], axis=1)
        for j in range(2)]).astype(BF16)

    full = lambda shape: pl.BlockSpec(shape, lambda b, s: (0,) * len(shape))
    kern = functools.partial(_mixer_kernel, w_conv=w_conv, w_sgu=w_sgu, w_lru=w_lru)
    return pl.pallas_call(
        kern,
        grid=(bsz, seq // rows),
        in_specs=[
            pl.BlockSpec((None, rows, d), lambda b, s: (b, s, 0)),
            pl.BlockSpec((None, N_MOD, d), lambda b, s: (b, 0, 0)),
            full((1, d)),
            full((d, d_in)),
            full(conv_a_w.shape),
            full((n_sgu_heads * CHUNK, CHUNK)),
            full((CHUNK, w_sgu)),
            full((1, w_sgu)),
            full((1, w_sgu)),
            full(conv_c_w.shape),
            full((1, w_lru)),
            full((2, half, 2 * half)),
            full((2, w_lru)),
            full((1, w_lru)),
            full((1, d)),
            full((d, d)),
        ],
        out_specs=pl.BlockSpec((None, rows, d), lambda b, s: (b, s, 0)),
        out_shape=jax.ShapeDtypeStruct((bsz, seq, d), F32),
        scratch_shapes=[
            pltpu.VMEM((rows + SUBLANES, w_conv), F32),
            pltpu.VMEM((rows + SUBLANES, w_lru), F32),
            pltpu.VMEM((SUBLANES, w_lru), F32),
            pltpu.VMEM((rows, w_lru), F32),
            pltpu.VMEM((rows, w_lru), F32),
        ],
        compiler_params=pltpu.CompilerParams(
            dimension_semantics=("arbitrary", "arbitrary"), vmem_limit_bytes=VMEM_LIMIT_BYTES),
        name="mixer",
    )(x, mod_l, norm1_g.reshape(1, d), w_in.astype(BF16), conv_a_w,
      sgu_w.reshape(n_sgu_heads * CHUNK, CHUNK), sgu_bias, sgu_ln_g.reshape(1, w_sgu),
      sgu_ln_b.reshape(1, w_sgu), conv_c_w, conv_c_b.reshape(1, w_lru), gate_w,
      rglru_gate_b.reshape(2, w_lru), rglru_lambda.reshape(1, w_lru), mix_norm_g.reshape(1, d),
      w_out.astype(BF16))


def _pack_bf16_pairs(a, b):
    ua = lax.bitcast_convert_type(a, U32)
    ub = lax.bitcast_convert_type(b, U32)
    return ua | (ub >> 16)


def _unpack_bf16_pairs(p):
    a = lax.bitcast_convert_type(p & jnp.uint32(0xFFFF0000), F32)
    b = lax.bitcast_convert_type(p << 16, F32)
    return a, b


def _stage_rows(n_exp):
    return _round_up(TOP_K * ROUTE_ROWS + n_exp * (SEG_ALIGN - 1), ROUTE_ROWS)


def _modulated_norm(x, g, scale, shift):
    return (_rms_normalize(x) * g) * (1.0 + scale) + shift


def _split_bf16(v):
    hi = v.astype(BF16)
    return hi, (v - hi.astype(F32)).astype(BF16)


def _router_kernel(x_ref, mod_ref, n2g_ref, rwt_ref, rb_ref, xs_ref, pos_ref, cnt_ref):
    for wi in range(xs_ref.shape[0]):
        sl = slice(wi * ROUTE_ROWS, (wi + 1) * ROUTE_ROWS)
        _route_window(x_ref[sl, :], mod_ref, n2g_ref, rwt_ref, rb_ref,
                      xs_ref.at[wi], pos_ref.at[:, sl], cnt_ref.at[wi])


def _route_window(x, mod_ref, n2g_ref, rwt_ref, rb_ref, xs_ref, pos_ref, cnt_ref):
    rows, d = x.shape
    n_exp = rwt_ref.shape[0]
    stage = xs_ref.shape[0]
    half = d // 2
    nt = (((1,), (1,)), ((), ()))

    h2 = _modulated_norm(x, n2g_ref[...], mod_ref[4:5, :], mod_ref[3:4, :])
    h_hi, h_lo = _split_bf16(h2)
    rw_hi, rw_lo = _split_bf16(rwt_ref[...])
    logits = (lax.dot_general(rw_hi, h_hi, nt, preferred_element_type=F32)
              + lax.dot_general(rw_hi, h_lo, nt, preferred_element_type=F32)
              + lax.dot_general(rw_lo, h_hi, nt, preferred_element_type=F32))
    logits = logits + rb_ref[...]

    e_iota = lax.broadcasted_iota(I32, (n_exp, rows), 0).astype(F32)
    work = logits
    tops, hots, idxs = [], [], []
    for k in range(TOP_K):
        m = jnp.max(work, axis=0, keepdims=True)
        idx = jnp.min(jnp.where(work == m, e_iota, float(n_exp)), axis=0, keepdims=True)
        hot = e_iota == idx
        work = jnp.where(hot, -jnp.inf, work)
        tops.append(m)
        hots.append(hot)
        idxs.append(idx)

    p = [jnp.exp(t - tops[0]) for t in tops]
    denom = p[0] + p[1] + p[2] + p[3]

    sel = jnp.zeros((n_exp, rows), F32)
    for hot in hots:
        sel = sel + hot.astype(F32)
    s_io = lax.broadcasted_iota(I32, (rows, rows), 0)
    t_io = lax.broadcasted_iota(I32, (rows, rows), 1)
    before = (s_io < t_io).astype(BF16)
    rank_e = jnp.dot(sel.astype(BF16), before, preferred_element_type=F32)
    n_e = jnp.sum(sel, axis=1, keepdims=True)
    seg_len = jnp.floor((n_e + (SEG_ALIGN - 1)) * (1.0 / SEG_ALIGN)) * SEG_ALIGN
    incl = jnp.broadcast_to(seg_len, (n_exp, LANES))
    e_row = lax.broadcasted_iota(I32, (n_exp, LANES), 0)
    step = 1
    while step < n_exp:
        incl = incl + jnp.where(e_row >= step, pltpu.roll(incl, step, 0), 0.0)
        step *= 2
    dest_e = (incl[:, 0:1] - seg_len) + rank_e

    r_iota = lax.broadcasted_iota(I32, (stage, rows), 0).astype(F32)
    onehot = None
    side = []
    for k in range(TOP_K):
        pos_k = jnp.sum(jnp.where(hots[k], dest_e, 0.0), axis=0, keepdims=True)
        pos_ref[k:k + 1, :] = pos_k.astype(I32)
        eq = r_iota == pos_k
        onehot = eq if onehot is None else (onehot | eq)
        side.append(p[k] / denom)
    perm = jnp.where(onehot, 1.0, 0.0).astype(BF16)

    w_rows = jnp.concatenate(side, axis=0)
    w_hi = w_rows.astype(BF16).astype(F32)
    side_rows = jnp.concatenate(
        [w_hi, w_rows - w_hi, jnp.concatenate(idxs, axis=0),
         jnp.zeros((LANES - 3 * TOP_K, rows), F32)], axis=0)
    side_cols = jnp.transpose(side_rows).astype(BF16)
    staged = jnp.dot(perm, jnp.concatenate([h_hi, side_cols], axis=1),
                     preferred_element_type=F32)
    xs_ref[:, 0:half] = _pack_bf16_pairs(staged[:, :half], staged[:, half:d])
    xs_ref[:, half:] = lax.bitcast_convert_type(staged[:, d:], U32)
    cnt_ref[...] = jnp.broadcast_to(n_e, (n_exp, LANES)).astype(I32)


def _router(x1, mod_l, norm2_g, router_w, router_b):
    bsz, seq, d = x1.shape
    n_tok = bsz * seq
    n_exp = router_w.shape[1]
    n_win = n_tok // ROUTE_ROWS
    wins = ROUTE_STEP_WINDOWS
    rows = wins * ROUTE_ROWS
    tiles_per_seq = seq // rows
    stage = _stage_rows(n_exp)
    return pl.pallas_call(
        _router_kernel,
        grid=(n_win // wins,),
        in_specs=[
            pl.BlockSpec((rows, d), lambda i: (i, 0)),
            pl.BlockSpec((None, N_MOD, d), lambda i: (i // tiles_per_seq, 0, 0)),
            pl.BlockSpec((1, d), lambda i: (0, 0)),
            pl.BlockSpec((n_exp, d), lambda i: (0, 0)),
            pl.BlockSpec((n_exp, 1), lambda i: (0, 0)),
        ],
        out_specs=[
            pl.BlockSpec((wins, stage, d // 2 + LANES), lambda i: (i, 0, 0)),
            pl.BlockSpec((TOP_K, rows), lambda i: (0, i)),
            pl.BlockSpec((wins, n_exp, LANES), lambda i: (i, 0, 0)),
        ],
        out_shape=[
            jax.ShapeDtypeStruct((n_win, stage, d // 2 + LANES), U32),
            jax.ShapeDtypeStruct((TOP_K, n_tok), I32),
            jax.ShapeDtypeStruct((n_win, n_exp, LANES), I32),
        ],
        compiler_params=pltpu.CompilerParams(
            dimension_semantics=("arbitrary",), vmem_limit_bytes=VMEM_LIMIT_BYTES),
        name="router",
    )(x1.reshape(n_tok, d), mod_l, norm2_g.reshape(1, d), router_w.T, router_b.reshape(n_exp, 1))


def _expert_kernel(be_sm, first_sm, nact_sm, valid_sm, plo_sm, phi_sm, psrc_sm, pdst_sm, pn_sm, csrc_sm,
                   cn_sm, used_sm, xs_ref, w1_ref, b1_ref, w2_ref, b2_ref, ys_ref,
                   w1b_ref, w2b_ref, xbuf, ybuf, zbuf, in_sem, out_sem, tail_sem, *, n_win, stage):
    g = pl.program_id(0)
    n_steps = pl.num_programs(0)
    nact = nact_sm[0]
    d_ff = w2_ref.shape[0]
    half = ybuf.shape[2]

    def pieces(blk, fn):
        lo, hi = plo_sm[blk], phi_sm[blk]
        carried = cn_sm[blk]

        @pl.when(carried > 0)
        def _():
            fn(pl.multiple_of(csrc_sm[blk], SEG_ALIGN), 0, pl.multiple_of(carried, SEG_ALIGN))

        def body(i, c):
            for j in range(PIECE_UNROLL):
                p = lo + i * PIECE_UNROLL + j
                n = pn_sm[p]

                @pl.when((p < hi) & (n > 0))
                def _():
                    fn(pl.multiple_of(psrc_sm[p], SEG_ALIGN), pl.multiple_of(pdst_sm[p], SEG_ALIGN),
                       pl.multiple_of(n, SEG_ALIGN))
            return c
        trips = lax.shift_right_logical(hi - lo + (PIECE_UNROLL - 1), PIECE_UNROLL.bit_length() - 1)
        lax.fori_loop(0, trips, body, 0)

    def in_copy(buf, src, dst, n):
        return pltpu.make_async_copy(
            xs_ref.at[pl.ds(src, n), :], xbuf.at[buf, pl.ds(dst, n), :], in_sem.at[buf])

    def out_copy(buf, src, dst, n):
        return pltpu.make_async_copy(
            ybuf.at[buf, pl.ds(dst, n), :], ys_ref.at[pl.ds(src, n), :], out_sem.at[buf])

    def valid_rows(blk):
        return pl.multiple_of(valid_sm[blk], SEG_ALIGN)

    def tail_copy(w):
        used = pl.multiple_of(used_sm[w], SEG_ALIGN)
        n = pl.multiple_of(stage - used, SEG_ALIGN)
        row = pl.multiple_of(w * stage + used, SEG_ALIGN)
        return pltpu.make_async_copy(zbuf.at[pl.ds(0, n), :], ys_ref.at[pl.ds(row, n), :], tail_sem)

    def for_windows(fn):
        def body(w, c):
            fn(w)
            return c
        lax.fori_loop(0, n_win, body, 0)

    @pl.when(g == 0)
    def _():
        xbuf[...] = jnp.zeros_like(xbuf)
        zbuf[...] = jnp.zeros_like(zbuf)
        for_windows(lambda w: tail_copy(w).start())
        pieces(0, lambda *a: in_copy(0, *a).start())

    @pl.when(g + 1 < nact)
    def _():
        pieces(g + 1, lambda *a: in_copy((g + 1) % 2, *a).start())

    @pl.when((g >= 2) & (g - 2 < nact))
    def _():
        out_copy(g % 2, 0, 0, valid_rows(g - 2)).wait()

    @pl.when(first_sm[g] == 1)
    def _():
        w1b_ref[...] = w1_ref[...].astype(BF16)
        w2b_ref[...] = w2_ref[...].astype(BF16)

    @pl.when(g < nact)
    def _():
        buf = g % 2
        in_copy(buf, 0, 0, valid_rows(g)).wait()
        packed = xbuf[buf]
        keep = lax.broadcasted_iota(I32, (EXPERT_ROWS, 1), 0) < valid_sm[g]
        xa, xb = _unpack_bf16_pairs(packed[:, :half])
        x = jnp.where(keep, jnp.concatenate([xa, xb], axis=-1), 0.0).astype(BF16)
        side = lax.bitcast_convert_type(packed[:, half:], F32)
        expert = be_sm[g].astype(F32)
        wcol = jnp.zeros((EXPERT_ROWS, 1), F32)
        for k in range(TOP_K):
            w_k = side[:, k:k + 1] + side[:, TOP_K + k:TOP_K + k + 1]
            wcol = jnp.where(side[:, 2 * TOP_K + k:2 * TOP_K + k + 1] == expert, w_k, wcol)
        wcol = jnp.where(keep, wcol, 0.0)
        hgu = jnp.dot(x, w1b_ref[...], preferred_element_type=F32) + b1_ref[...]
        gt = jnp.minimum(hgu[:, :d_ff], SWIGLU_LIMIT)
        up = jnp.clip(hgu[:, d_ff:], -SWIGLU_LIMIT, SWIGLU_LIMIT)
        act = (up + 1.0) * (gt * jax.nn.sigmoid(SWIGLU_ALPHA * gt))
        y = (jnp.dot(act.astype(BF16), w2b_ref[...], preferred_element_type=F32) + b2_ref[...]) * wcol
        y = y.astype(BF16).astype(F32)
        ybuf[buf] = _pack_bf16_pairs(y[:, :half], y[:, half:])
        pieces(g, lambda *a: out_copy(buf, *a).start())

    @pl.when(g == n_steps - 1)
    def _():
        for back in (1, 0):
            blk = g - back

            @pl.when((blk >= 0) & (blk < nact))
            def _():
                out_copy(blk % 2, 0, 0, valid_rows(blk)).wait()
        for_windows(lambda w: tail_copy(w).wait())


def _experts(xs, tables, layer, w1, b1, w2, b2):
    n_win, stage, xw = xs.shape
    depth, n_exp, d, d_hid = w1.shape
    d_ff = w2.shape[2]
    half = d // 2
    n_blocks = tables[0].shape[0]
    wmap = lambda g, be, *_: (layer, be[g], 0, 0)
    grid_spec = pltpu.PrefetchScalarGridSpec(
        num_scalar_prefetch=len(tables),
        grid=(n_blocks,),
        in_specs=[
            pl.BlockSpec(memory_space=pl.ANY),
            pl.BlockSpec((None, None, d, d_hid), wmap),
            pl.BlockSpec((None, None, 1, d_hid), wmap),
            pl.BlockSpec((None, None, d_ff, d), wmap),
            pl.BlockSpec((None, None, 1, d), wmap),
        ],
        out_specs=pl.BlockSpec(memory_space=pl.ANY),
        scratch_shapes=[
            pltpu.VMEM((d, d_hid), BF16),
            pltpu.VMEM((d_ff, d), BF16),
            pltpu.VMEM((2, EXPERT_ROWS, xw), U32),
            pltpu.VMEM((2, EXPERT_ROWS, half), U32),
            pltpu.VMEM((stage - TOP_K * ROUTE_ROWS, half), U32),
            pltpu.SemaphoreType.DMA((2,)),
            pltpu.SemaphoreType.DMA((2,)),
            pltpu.SemaphoreType.DMA,
        ],
    )
    ys = pl.pallas_call(
        functools.partial(_expert_kernel, n_win=n_win, stage=stage),
        grid_spec=grid_spec,
        out_shape=jax.ShapeDtypeStruct((n_win * stage, half), U32),
        compiler_params=pltpu.CompilerParams(
            dimension_semantics=("arbitrary",), vmem_limit_bytes=VMEM_LIMIT_BYTES),
        name="experts",
    )(*tables, xs.reshape(n_win * stage, xw), w1, b1.reshape(depth, n_exp, 1, d_hid), w2,
      b2.reshape(depth, n_exp, 1, d))
    return ys.reshape(n_win, stage, half)


def _combine_kernel(x_ref, mod_ref, pos_ref, fng_ref, ys_ref, o_ref, *, final_norm):
    rows = x_ref.shape[0]
    stage = ys_ref.shape[0]
    pos_rows = jnp.concatenate(
        [pos_ref[...].astype(F32), jnp.full((LANES - TOP_K, rows), -1.0, F32)], axis=0)
    pos_t = jnp.transpose(pos_rows)
    c_iota = lax.broadcasted_iota(I32, (rows, stage), 1).astype(F32)
    hit = c_iota == pos_t[:, 0:1]
    for k in range(1, TOP_K):
        hit = hit | (c_iota == pos_t[:, k:k + 1])
    unperm = jnp.where(hit, 1.0, 0.0).astype(BF16)
    ya, yb = _unpack_bf16_pairs(ys_ref[...])
    moe = jnp.concatenate(
        [jnp.dot(unperm, ya.astype(BF16), preferred_element_type=F32),
         jnp.dot(unperm, yb.astype(BF16), preferred_element_type=F32)], axis=-1)
    out = x_ref[...] + mod_ref[5:6, :] * moe
    if final_norm:
        out = _rms_normalize(out) * fng_ref[...]
    o_ref[...] = out


def _combine(x1, mod_l, pos, ys, final_norm_g, final_norm):
    bsz, seq, d = x1.shape
    n_tok = bsz * seq
    rows = ROUTE_ROWS
    tiles_per_seq = seq // rows
    n_win, stage, half = ys.shape
    out = pl.pallas_call(
        functools.partial(_combine_kernel, final_norm=final_norm),
        grid=(n_win,),
        in_specs=[
            pl.BlockSpec((rows, d), lambda i: (i, 0)),
            pl.BlockSpec((None, N_MOD, d), lambda i: (i // tiles_per_seq, 0, 0)),
            pl.BlockSpec((TOP_K, rows), lambda i: (0, i)),
            pl.BlockSpec((1, d), lambda i: (0, 0)),
            pl.BlockSpec((None, stage, half), lambda i: (i, 0, 0)),
        ],
        out_specs=pl.BlockSpec((rows, d), lambda i: (i, 0)),
        out_shape=jax.ShapeDtypeStruct((n_tok, d), F32),
        compiler_params=pltpu.CompilerParams(
            dimension_semantics=("arbitrary",), vmem_limit_bytes=VMEM_LIMIT_BYTES),
        name="combine",
    )(x1.reshape(n_tok, d), mod_l, pos, final_norm_g.reshape(1, d), ys)
    return out.reshape(bsz, seq, d)


def _segment_tables(cnt, stage):
    n_win, n_exp = cnt.shape
    seg_len = _round_up(cnt, SEG_ALIGN)
    seg_src = jnp.cumsum(seg_len, axis=1) - seg_len
    used = jnp.sum(seg_len, axis=1)
    seg_off = jnp.cumsum(seg_len, axis=0) - seg_len
    reg_len = jnp.sum(seg_len, axis=0)
    reg_pad = _round_up(reg_len, EXPERT_ROWS)
    pad_end = jnp.cumsum(reg_pad)
    base = pad_end - reg_pad
    max_rows = n_win * (TOP_K * ROUTE_ROWS + n_exp * (SEG_ALIGN - 1)) + n_exp * (EXPERT_ROWS - SEG_ALIGN)
    n_blocks = -(-max_rows // EXPERT_ROWS)
    n_active = pad_end[-1] // EXPERT_ROWS
    gidx = jnp.arange(n_blocks, dtype=I32)
    blk_row = (gidx * EXPERT_ROWS)[:, None]
    in_region = (base[None, :] <= blk_row) & (blk_row < pad_end[None, :])
    valid = jnp.sum(jnp.where(in_region, jnp.clip((base + reg_len)[None, :] - blk_row, 0, EXPERT_ROWS), 0), axis=1)
    first = jnp.any(in_region & (base[None, :] == blk_row), axis=1)
    last_row = (jnp.minimum(gidx, n_active - 1) * EXPERT_ROWS)[:, None]
    block_e = jnp.minimum(jnp.sum(pad_end[None, :] <= last_row, axis=1), n_exp - 1)

    glob = (base[None, :] + seg_off).T.reshape(-1)
    seg_n = seg_len.T.reshape(-1)
    src = (jnp.arange(n_win, dtype=I32)[:, None] * stage + seg_src).T.reshape(-1)
    head_blk = glob // EXPERT_ROWS
    head_dst = glob % EXPERT_ROWS
    head_n = jnp.minimum(seg_n, EXPERT_ROWS - head_dst)
    carry_n = seg_n - head_n
    p_lo = jnp.sum(head_blk[None, :] < gidx[:, None], axis=1)
    p_hi = jnp.sum(head_blk[None, :] <= gidx[:, None], axis=1)
    carried = (head_blk[None, :] + 1 == gidx[:, None]) & (carry_n[None, :] > 0)
    c_n = jnp.sum(jnp.where(carried, carry_n[None, :], 0), axis=1)
    c_src = jnp.sum(jnp.where(carried, (src + head_n)[None, :], 0), axis=1)
    p_src, p_dst, p_n = (jnp.pad(t, (0, PIECE_UNROLL)) for t in (src, head_dst, head_n))
    tables = (block_e, first, n_active.reshape(1), valid, p_lo, p_hi, p_src, p_dst, p_n, c_src, c_n, used)
    return tuple(t.astype(I32) for t in tables)


def _moe_layer(x1, mod_l, norm2_g, router_w, router_b, layer, w1, b1, w2, b2, final_norm_g, final_norm):
    xs, pos, cnt = _router(x1, mod_l, norm2_g, router_w, router_b)
    tables = _segment_tables(cnt[:, :, 0], xs.shape[1])
    ys = _experts(xs, tables, layer, w1, b1, w2, b2)
    return _combine(x1, mod_l, pos, ys, final_norm_g, final_norm)


def kernel(x, c, norm1_g, norm2_g, w_ada, b_ada, w_in, conv_a_w, sgu_w, sgu_b, sgu_ln_g, sgu_ln_b,
           conv_c_w, conv_c_b, rglru_gate_w, rglru_gate_b, rglru_lambda, mix_norm_g, w_out, router_w,
           router_b, expert_w1, expert_b1, expert_w2, expert_b2, final_norm_g):
    depth = w_ada.shape[0]
    mod = _adaln_modulation(c, w_ada, b_ada)
    for l in range(depth):
        x1 = _mixer(x, mod[l], norm1_g[l], w_in[l], conv_a_w[l], sgu_w[l], sgu_b[l], sgu_ln_g[l],
                    sgu_ln_b[l], conv_c_w[l], conv_c_b[l], rglru_gate_w[l], rglru_gate_b[l],
                    rglru_lambda[l], mix_norm_g[l], w_out[l])
        x = _moe_layer(x1, mod[l], norm2_g[l], router_w[l], router_b[l], l, expert_w1, expert_b1,
                       expert_w2, expert_b2, final_norm_g, l == depth - 1)
    return x
```

```python
import functools

import jax
import jax.numpy as jnp
from jax import lax
from jax.experimental import pallas as pl
from jax.experimental.pallas import tpu as pltpu

F32, BF16, I32, U32 = jnp.float32, jnp.bfloat16, jnp.int32, jnp.uint32

HEAD_DIM = 64
CHUNK = 128
RGLRU_C = 8.0
TOP_K = 4
SWIGLU_LIMIT = 7.0
SWIGLU_ALPHA = 1.702
N_MOD = 6
EPS = 1e-6

SUBLANES = 8
LANES = 128
VMEM_LIMIT_BYTES = 56 * 1024 * 1024

MIX_ROWS = 256
ROUTE_ROWS = 256
ROUTE_STEP_WINDOWS = 2
PIECE_UNROLL = 4
INLINE_PIECES = 12
EXPERT_ROWS = 256
SEG_ALIGN = SUBLANES
MOD_COLS = 1024


def _rms_normalize(y):
    return y * lax.rsqrt(jnp.mean(y * y, axis=-1, keepdims=True) + EPS)


def _round_up(x, m):
    return (x + m - 1) // m * m


def _mod_kernel(c_ref, w_ref, b_ref, o_ref):
    o_ref[...] = (
        jnp.dot(c_ref[...], w_ref[...], preferred_element_type=F32, precision=lax.Precision.HIGHEST)
        + b_ref[...]
    )


def _adaln_modulation(c, w_ada, b_ada):
    depth, d, n = w_ada.shape
    bsz = c.shape[0]
    rows = -(-bsz // SUBLANES) * SUBLANES
    c_pad = jnp.pad(c, ((0, rows - bsz), (0, 0)))
    out = pl.pallas_call(
        _mod_kernel,
        grid=(depth, n // MOD_COLS),
        in_specs=[
            pl.BlockSpec((rows, d), lambda l, j: (0, 0)),
            pl.BlockSpec((None, d, MOD_COLS), lambda l, j: (l, 0, j)),
            pl.BlockSpec((None, 1, MOD_COLS), lambda l, j: (l, 0, j)),
        ],
        out_specs=pl.BlockSpec((None, rows, MOD_COLS), lambda l, j: (l, 0, j)),
        out_shape=jax.ShapeDtypeStruct((depth, rows, n), F32),
        name="adaln_mod",
    )(c_pad, w_ada, b_ada.reshape(depth, 1, n))
    return out[:, :bsz].reshape(depth, bsz, N_MOD, d)


def _mixer_kernel(x_ref, mod_ref, n1g_ref, win_ref, caw_ref, sguw_ref, sgub_ref, lng_ref, lnb_ref,
                  ccw_ref, ccb_ref, gw_ref, gb_ref, lam_ref, mng_ref, wout_ref,
                  o_ref, cvh_ref, zxh_ref, hc_ref, a_scr, b_scr, *, w_conv, w_sgu, w_lru):
    rows = x_ref.shape[0]
    hist = SUBLANES

    @pl.when(pl.program_id(1) == 0)
    def _():
        cvh_ref[0:hist, :] = jnp.zeros((hist, w_conv), F32)
        zxh_ref[0:hist, :] = jnp.zeros((hist, w_lru), F32)
        hc_ref[...] = jnp.zeros_like(hc_ref)

    x = x_ref[...]
    shift1, scale1, gate1 = mod_ref[0:1, :], mod_ref[1:2, :], mod_ref[2:3, :]
    h = _rms_normalize(x) * n1g_ref[...]
    h = h * (1.0 + scale1) + shift1
    z = jnp.dot(h.astype(BF16), win_ref[...], preferred_element_type=F32)

    o = 0
    z_b = z[:, o:o + w_conv]; o += w_conv
    z_c = z[:, o:o + w_conv]; o += w_conv
    z_v = z[:, o:o + w_conv]; o += w_conv
    z_su = z[:, o:o + w_sgu]; o += w_sgu
    z_sv = z[:, o:o + w_sgu]; o += w_sgu
    z_g = z[:, o:o + w_lru]; o += w_lru
    z_x = z[:, o:o + w_lru]

    cv = z_c * z_v
    cvh_ref[hist:hist + rows, :] = cv
    conv = (caw_ref[0:1, :] * cvh_ref[hist - 2:hist - 2 + rows, :]
            + caw_ref[1:2, :] * cvh_ref[hist - 1:hist - 1 + rows, :]
            + caw_ref[2:3, :] * cv)
    cvh_ref[0:hist, :] = cvh_ref[rows:rows + hist, :]
    y_conv = z_b * conv

    u = jax.nn.gelu(z_su)
    v = jax.nn.gelu(z_sv)
    vc = v - jnp.mean(v, axis=-1, keepdims=True)
    vn = vc * lax.rsqrt(jnp.mean(vc * vc, axis=-1, keepdims=True) + EPS)
    vn = (vn * lng_ref[...] + lnb_ref[...]).astype(BF16)
    n_heads = w_sgu // HEAD_DIM
    wr = lax.broadcasted_iota(I32, (n_heads * CHUNK, CHUNK), 0)
    wc = lax.broadcasted_iota(I32, (n_heads * CHUNK, CHUNK), 1)
    w_causal = jnp.where(wc <= (wr & (CHUNK - 1)), sguw_ref[...], 0.0).astype(BF16)
    lane = lax.broadcasted_iota(I32, (CHUNK, w_sgu), 1)
    mixed = []
    for ci in range(rows // CHUNK):
        res = jnp.dot(w_causal, vn[ci * CHUNK:(ci + 1) * CHUNK, :], preferred_element_type=F32)
        m = res[0:CHUNK, :]
        for hh in range(1, n_heads):
            m = jnp.where(lane >= hh * HEAD_DIM, res[hh * CHUNK:(hh + 1) * CHUNK, :], m)
        mixed.append(m + sgub_ref[...])
    y_sgu = u * jnp.concatenate(mixed, axis=0)

    zxh_ref[hist:hist + rows, :] = z_x
    xr = (ccw_ref[0:1, :] * zxh_ref[hist - 3:hist - 3 + rows, :]
          + ccw_ref[1:2, :] * zxh_ref[hist - 2:hist - 2 + rows, :]
          + ccw_ref[2:3, :] * zxh_ref[hist - 1:hist - 1 + rows, :]
          + ccw_ref[3:4, :] * z_x + ccb_ref[...])
    zxh_ref[0:hist, :] = zxh_ref[rows:rows + hist, :]
    xr_b = xr.astype(BF16)
    half = w_lru // 2
    g0 = jnp.dot(xr_b[:, :half], gw_ref[0], preferred_element_type=F32)
    g1 = jnp.dot(xr_b[:, half:], gw_ref[1], preferred_element_type=F32)
    r_gate = jax.nn.sigmoid(jnp.concatenate([g0[:, :half], g1[:, :half]], axis=-1) + gb_ref[0:1, :])
    i_gate = jax.nn.sigmoid(jnp.concatenate([g0[:, half:], g1[:, half:]], axis=-1) + gb_ref[1:2, :])
    neg_lam = -lam_ref[...]
    softplus = jnp.maximum(neg_lam, 0.0) + jnp.log1p(jnp.exp(-jnp.abs(neg_lam)))
    log_a = (-RGLRU_C) * r_gate * softplus
    a = jnp.exp(log_a)
    a_scr[...] = a
    one_m_a2 = 1.0 - a * a
    b_scr[...] = xr * i_gate * jnp.where(one_m_a2 > 0.0, one_m_a2 * lax.rsqrt(one_m_a2), 0.0)

    row8 = lax.broadcasted_iota(I32, (SUBLANES, w_lru), 0)

    def scan_group(g, carry):
        r0 = pl.multiple_of(g * SUBLANES, SUBLANES)
        a8 = a_scr[pl.ds(r0, SUBLANES), :]
        b8 = b_scr[pl.ds(r0, SUBLANES), :]
        for d in (1, 2, 4):
            keep = row8 >= d
            b8 = jnp.where(keep, a8 * pltpu.roll(b8, d, 0) + b8, b8)
            a8 = jnp.where(keep, a8 * pltpu.roll(a8, d, 0), a8)
        h8 = a8 * carry + b8
        b_scr[pl.ds(r0, SUBLANES), :] = h8
        return jnp.broadcast_to(h8[SUBLANES - 1:SUBLANES, :], (SUBLANES, w_lru))

    hc_ref[...] = lax.fori_loop(0, rows // SUBLANES, scan_group, hc_ref[...], unroll=True)
    y_lru = jax.nn.gelu(z_g) * b_scr[...]

    y = jnp.concatenate([_rms_normalize(y_conv), _rms_normalize(y_sgu), _rms_normalize(y_lru)], axis=-1)
    y = (y * mng_ref[...]).astype(BF16)
    o_ref[...] = x + gate1 * jnp.dot(y, wout_ref[...], preferred_element_type=F32)


def _mixer(x, mod_l, norm1_g, w_in, conv_a_w, sgu_w, sgu_b, sgu_ln_g, sgu_ln_b, conv_c_w, conv_c_b,
           rglru_gate_w, rglru_gate_b, rglru_lambda, mix_norm_g, w_out):
    bsz, seq, d = x.shape
    w_conv = conv_a_w.shape[1]
    w_sgu = sgu_ln_g.shape[0]
    w_lru = rglru_lambda.shape[0]
    d_in = w_in.shape[1]
    n_sgu_heads = sgu_w.shape[0]
    n_lru_heads = rglru_gate_w.shape[1]
    rows = MIX_ROWS
    half = w_lru // 2

    sgu_bias = jnp.repeat(sgu_b.T, HEAD_DIM, axis=1)
    eye = jnp.eye(n_lru_heads, dtype=F32)
    bd = jnp.einsum("ghij,hk->ghikj", rglru_gate_w, eye).reshape(2, w_lru, w_lru)
    gate_w = jnp.stack([
        jnp.concatenate([bd[0, j * half:(j + 1) * half, j * half:(j + 1) * half],
                         bd[1, j * half:(j + 1) * half, j * half:(j + 1) * half]], axis=1)
        for j in range(2)]).astype(BF16)

    full = lambda shape: pl.BlockSpec(shape, lambda b, s: (0,) * len(shape))
    kern = functools.partial(_mixer_kernel, w_conv=w_conv, w_sgu=w_sgu, w_lru=w_lru)
    return pl.pallas_call(
        kern,
        grid=(bsz, seq // rows),
        in_specs=[
            pl.BlockSpec((None, rows, d), lambda b, s: (b, s, 0)),
            pl.BlockSpec((None, N_MOD, d), lambda b, s: (b, 0, 0)),
            full((1, d)),
            full((d, d_in)),
            full(conv_a_w.shape),
            full((n_sgu_heads * CHUNK, CHUNK)),
            full((CHUNK, w_sgu)),
            full((1, w_sgu)),
            full((1, w_sgu)),
            full(conv_c_w.shape),
            full((1, w_lru)),
            full((2, half, 2 * half)),
            full((2, w_lru)),
            full((1, w_lru)),
            full((1, d)),
            full((d, d)),
        ],
        out_specs=pl.BlockSpec((None, rows, d), lambda b, s: (b, s, 0)),
        out_shape=jax.ShapeDtypeStruct((bsz, seq, d), F32),
        scratch_shapes=[
            pltpu.VMEM((rows + SUBLANES, w_conv), F32),
            pltpu.VMEM((rows + SUBLANES, w_lru), F32),
            pltpu.VMEM((SUBLANES, w_lru), F32),
            pltpu.VMEM((rows, w_lru), F32),
            pltpu.VMEM((rows, w_lru), F32),
        ],
        compiler_params=pltpu.CompilerParams(
            dimension_semantics=("arbitrary", "arbitrary"), vmem_limit_bytes=VMEM_LIMIT_BYTES),
        name="mixer",
    )(x, mod_l, norm1_g.reshape(1, d), w_in.astype(BF16), conv_a_w,
      sgu_w.reshape(n_sgu_heads * CHUNK, CHUNK), sgu_bias, sgu_ln_g.reshape(1, w_sgu),
      sgu_ln_b.reshape(1, w_sgu), conv_c_w, conv_c_b.reshape(1, w_lru), gate_w,
      rglru_gate_b.reshape(2, w_lru), rglru_lambda.reshape(1, w_lru), mix_norm_g.reshape(1, d),
      w_out.astype(BF16))


def _pack_bf16_pairs(a, b):
    ua = lax.bitcast_convert_type(a, U32)
    ub = lax.bitcast_convert_type(b, U32)
    return ua | (ub >> 16)


def _unpack_bf16_pairs(p):
    a = lax.bitcast_convert_type(p & jnp.uint32(0xFFFF0000), F32)
    b = lax.bitcast_convert_type(p << 16, F32)
    return a, b


def _stage_rows(n_exp):
    return _round_up(TOP_K * ROUTE_ROWS + n_exp * (SEG_ALIGN - 1), ROUTE_ROWS)


def _modulated_norm(x, g, scale, shift):
    return (_rms_normalize(x) * g) * (1.0 + scale) + shift


def _split_bf16(v):
    hi = v.astype(BF16)
    return hi, (v - hi.astype(F32)).astype(BF16)


def _router_kernel(x_ref, mod_ref, n2g_ref, rwt_ref, rb_ref, xs_ref, pos_ref, cnt_ref):
    for wi in range(xs_ref.shape[0]):
        sl = slice(wi * ROUTE_ROWS, (wi + 1) * ROUTE_ROWS)
        _route_window(x_ref[sl, :], mod_ref, n2g_ref, rwt_ref, rb_ref,
                      xs_ref.at[wi], pos_ref.at[:, sl], cnt_ref.at[wi])


def _route_window(x, mod_ref, n2g_ref, rwt_ref, rb_ref, xs_ref, pos_ref, cnt_ref):
    rows, d = x.shape
    n_exp = rwt_ref.shape[0]
    stage = xs_ref.shape[0]
    half = d // 2
    nt = (((1,), (1,)), ((), ()))

    h2 = _modulated_norm(x, n2g_ref[...], mod_ref[4:5, :], mod_ref[3:4, :])
    h_hi, h_lo = _split_bf16(h2)
    rw_hi, rw_lo = _split_bf16(rwt_ref[...])
    logits = (lax.dot_general(rw_hi, h_hi, nt, preferred_element_type=F32)
              + lax.dot_general(rw_hi, h_lo, nt, preferred_element_type=F32)
              + lax.dot_general(rw_lo, h_hi, nt, preferred_element_type=F32))
    logits = logits + rb_ref[...]

    e_iota = lax.broadcasted_iota(I32, (n_exp, rows), 0).astype(F32)
    work = logits
    tops, hots, idxs = [], [], []
    for k in range(TOP_K):
        m = jnp.max(work, axis=0, keepdims=True)
        idx = jnp.min(jnp.where(work == m, e_iota, float(n_exp)), axis=0, keepdims=True)
        hot = e_iota == idx
        work = jnp.where(hot, -jnp.inf, work)
        tops.append(m)
        hots.append(hot)
        idxs.append(idx)

    p = [jnp.exp(t - tops[0]) for t in tops]
    denom = p[0] + p[1] + p[2] + p[3]

    sel = jnp.zeros((n_exp, rows), F32)
    for hot in hots:
        sel = sel + hot.astype(F32)
    s_io = lax.broadcasted_iota(I32, (rows, rows), 0)
    t_io = lax.broadcasted_iota(I32, (rows, rows), 1)
    before = (s_io < t_io).astype(BF16)
    rank_e = jnp.dot(sel.astype(BF16), before, preferred_element_type=F32)
    n_e = jnp.sum(sel, axis=1, keepdims=True)
    seg_len = jnp.floor((n_e + (SEG_ALIGN - 1)) * (1.0 / SEG_ALIGN)) * SEG_ALIGN
    incl = jnp.broadcast_to(seg_len, (n_exp, LANES))
    e_row = lax.broadcasted_iota(I32, (n_exp, LANES), 0)
    step = 1
    while step < n_exp:
        incl = incl + jnp.where(e_row >= step, pltpu.roll(incl, step, 0), 0.0)
        step *= 2
    dest_e = (incl[:, 0:1] - seg_len) + rank_e

    r_iota = lax.broadcasted_iota(I32, (stage, rows), 0).astype(F32)
    onehot = None
    side = []
    for k in range(TOP_K):
        pos_k = jnp.sum(jnp.where(hots[k], dest_e, 0.0), axis=0, keepdims=True)
        pos_ref[k:k + 1, :] = pos_k.astype(I32)
        eq = r_iota == pos_k
        onehot = eq if onehot is None else (onehot | eq)
        side.append(p[k] / denom)
    perm = jnp.where(onehot, 1.0, 0.0).astype(BF16)

    w_rows = jnp.concatenate(side, axis=0)
    w_hi = w_rows.astype(BF16).astype(F32)
    side_rows = jnp.concatenate(
        [w_hi, w_rows - w_hi, jnp.concatenate(idxs, axis=0),
         jnp.zeros((LANES - 3 * TOP_K, rows), F32)], axis=0)
    side_cols = jnp.transpose(side_rows).astype(BF16)
    staged = jnp.dot(perm, jnp.concatenate([h_hi, side_cols], axis=1),
                     preferred_element_type=F32)
    xs_ref[:, 0:half] = _pack_bf16_pairs(staged[:, :half], staged[:, half:d])
    xs_ref[:, half:] = lax.bitcast_convert_type(staged[:, d:], U32)
    cnt_ref[...] = jnp.broadcast_to(n_e, (n_exp, LANES)).astype(I32)


def _router(x1, mod_l, norm2_g, router_w, router_b):
    bsz, seq, d = x1.shape
    n_tok = bsz * seq
    n_exp = router_w.shape[1]
    n_win = n_tok // ROUTE_ROWS
    wins = ROUTE_STEP_WINDOWS
    rows = wins * ROUTE_ROWS
    tiles_per_seq = seq // rows
    stage = _stage_rows(n_exp)
    return pl.pallas_call(
        _router_kernel,
        grid=(n_win // wins,),
        in_specs=[
            pl.BlockSpec((rows, d), lambda i: (i, 0)),
            pl.BlockSpec((None, N_MOD, d), lambda i: (i // tiles_per_seq, 0, 0)),
            pl.BlockSpec((1, d), lambda i: (0, 0)),
            pl.BlockSpec((n_exp, d), lambda i: (0, 0)),
            pl.BlockSpec((n_exp, 1), lambda i: (0, 0)),
        ],
        out_specs=[
            pl.BlockSpec((wins, stage, d // 2 + LANES), lambda i: (i, 0, 0)),
            pl.BlockSpec((TOP_K, rows), lambda i: (0, i)),
            pl.BlockSpec((wins, n_exp, LANES), lambda i: (i, 0, 0)),
        ],
        out_shape=[
            jax.ShapeDtypeStruct((n_win, stage, d // 2 + LANES), U32),
            jax.ShapeDtypeStruct((TOP_K, n_tok), I32),
            jax.ShapeDtypeStruct((n_win, n_exp, LANES), I32),
        ],
        compiler_params=pltpu.CompilerParams(
            dimension_semantics=("arbitrary",), vmem_limit_bytes=VMEM_LIMIT_BYTES),
        name="router",
    )(x1.reshape(n_tok, d), mod_l, norm2_g.reshape(1, d), router_w.T, router_b.reshape(n_exp, 1))


def _expert_kernel(be_sm, first_sm, nact_sm, valid_sm, plo_sm, phi_sm, psrc_sm, pdst_sm, pn_sm, csrc_sm,
                   cn_sm, used_sm, xs_ref, w1_ref, b1_ref, w2_ref, b2_ref, ys_ref,
                   w1b_ref, w2b_ref, xbuf, ybuf, zbuf, in_sem, out_sem, tail_sem, *, n_win, stage):
    g = pl.program_id(0)
    n_steps = pl.num_programs(0)
    nact = nact_sm[0]
    d_ff = w2_ref.shape[0]
    half = ybuf.shape[2]

    def pieces(blk, fn, enabled=True, inline=0):
        lo, hi = plo_sm[blk], phi_sm[blk]
        carried = cn_sm[blk]

        @pl.when(enabled & (carried > 0))
        def _():
            fn(pl.multiple_of(csrc_sm[blk], SEG_ALIGN), 0, pl.multiple_of(carried, SEG_ALIGN))

        def entry(p):
            n = pn_sm[p]

            @pl.when(enabled & (p < hi) & (n > 0))
            def _():
                fn(pl.multiple_of(psrc_sm[p], SEG_ALIGN), pl.multiple_of(pdst_sm[p], SEG_ALIGN),
                   pl.multiple_of(n, SEG_ALIGN))

        for j in range(inline):
            entry(lo + j)

        def body(i, c):
            for j in range(PIECE_UNROLL):
                entry(lo + inline + i * PIECE_UNROLL + j)
            return c
        left = jnp.maximum(hi - lo - inline, 0)
        trips = lax.shift_right_logical(left + (PIECE_UNROLL - 1), PIECE_UNROLL.bit_length() - 1)
        lax.fori_loop(0, jnp.where(enabled, trips, 0), body, 0)

    def in_copy(buf, src, dst, n):
        return pltpu.make_async_copy(
            xs_ref.at[pl.ds(src, n), :], xbuf.at[buf, pl.ds(dst, n), :], in_sem.at[buf])

    def out_copy(buf, src, dst, n):
        return pltpu.make_async_copy(
            ybuf.at[buf, pl.ds(dst, n), :], ys_ref.at[pl.ds(src, n), :], out_sem.at[buf])

    def valid_rows(blk):
        return pl.multiple_of(valid_sm[blk], SEG_ALIGN)

    def tail_copy(w):
        used = pl.multiple_of(used_sm[w], SEG_ALIGN)
        n = pl.multiple_of(stage - used, SEG_ALIGN)
        row = pl.multiple_of(w * stage + used, SEG_ALIGN)
        return pltpu.make_async_copy(zbuf.at[pl.ds(0, n), :], ys_ref.at[pl.ds(row, n), :], tail_sem)

    def for_windows(fn):
        def body(w, c):
            fn(w)
            return c
        lax.fori_loop(0, n_win, body, 0)

    @pl.when(g == 0)
    def _():
        xbuf[...] = jnp.zeros_like(xbuf)
        zbuf[...] = jnp.zeros_like(zbuf)
        for_windows(lambda w: tail_copy(w).start())
        pieces(0, lambda *a: in_copy(0, *a).start())

    @pl.when((g >= 2) & (g - 2 < nact))
    def _():
        out_copy(g % 2, 0, 0, valid_rows(g - 2)).wait()

    @pl.when(first_sm[g] == 1)
    def _():
        w1b_ref[...] = w1_ref[...].astype(BF16)
        w2b_ref[...] = w2_ref[...].astype(BF16)

    @pl.when(g < nact)
    def _():
        buf = g % 2
        pieces(g + 1, lambda *a: in_copy(1 - buf, *a).start(), enabled=g + 1 < nact, inline=INLINE_PIECES)
        in_copy(buf, 0, 0, valid_rows(g)).wait()
        packed = xbuf[buf]
        keep = lax.broadcasted_iota(I32, (EXPERT_ROWS, 1), 0) < valid_sm[g]
        xa, xb = _unpack_bf16_pairs(packed[:, :half])
        x = jnp.where(keep, jnp.concatenate([xa, xb], axis=-1), 0.0).astype(BF16)
        side = lax.bitcast_convert_type(packed[:, half:], F32)
        expert = be_sm[g].astype(F32)
        wcol = jnp.zeros((EXPERT_ROWS, 1), F32)
        for k in range(TOP_K):
            w_k = side[:, k:k + 1] + side[:, TOP_K + k:TOP_K + k + 1]
            wcol = jnp.where(side[:, 2 * TOP_K + k:2 * TOP_K + k + 1] == expert, w_k, wcol)
        wcol = jnp.where(keep, wcol, 0.0)
        hgu = jnp.dot(x, w1b_ref[...], preferred_element_type=F32) + b1_ref[...]
        gt = jnp.minimum(hgu[:, :d_ff], SWIGLU_LIMIT)
        up = jnp.clip(hgu[:, d_ff:], -SWIGLU_LIMIT, SWIGLU_LIMIT)
        act = (up + 1.0) * (gt * jax.nn.sigmoid(SWIGLU_ALPHA * gt))
        y = (jnp.dot(act.astype(BF16), w2b_ref[...], preferred_element_type=F32) + b2_ref[...]) * wcol
        y = y.astype(BF16).astype(F32)
        ybuf[buf] = _pack_bf16_pairs(y[:, :half], y[:, half:])
        pieces(g, lambda *a: out_copy(buf, *a).start(), inline=INLINE_PIECES)

    @pl.when(g == n_steps - 1)
    def _():
        for back in (1, 0):
            blk = g - back

            @pl.when((blk >= 0) & (blk < nact))
            def _():
                out_copy(blk % 2, 0, 0, valid_rows(blk)).wait()
        for_windows(lambda w: tail_copy(w).wait())


def _experts(xs, tables, layer, w1, b1, w2, b2):
    n_win, stage, xw = xs.shape
    depth, n_exp, d, d_hid = w1.shape
    d_ff = w2.shape[2]
    half = d // 2
    n_blocks = tables[0].shape[0]
    wmap = lambda g, be, *_: (layer, be[g], 0, 0)
    grid_spec = pltpu.PrefetchScalarGridSpec(
        num_scalar_prefetch=len(tables),
        grid=(n_blocks,),
        in_specs=[
            pl.BlockSpec(memory_space=pl.ANY),
            pl.BlockSpec((None, None, d, d_hid), wmap),
            pl.BlockSpec((None, None, 1, d_hid), wmap),
            pl.BlockSpec((None, None, d_ff, d), wmap),
            pl.BlockSpec((None, None, 1, d), wmap),
        ],
        out_specs=pl.BlockSpec(memory_space=pl.ANY),
        scratch_shapes=[
            pltpu.VMEM((d, d_hid), BF16),
            pltpu.VMEM((d_ff, d), BF16),
            pltpu.VMEM((2, EXPERT_ROWS, xw), U32),
            pltpu.VMEM((2, EXPERT_ROWS, half), U32),
            pltpu.VMEM((stage - TOP_K * ROUTE_ROWS, half), U32),
            pltpu.SemaphoreType.DMA((2,)),
            pltpu.SemaphoreType.DMA((2,)),
            pltpu.SemaphoreType.DMA,
        ],
    )
    ys = pl.pallas_call(
        functools.partial(_expert_kernel, n_win=n_win, stage=stage),
        grid_spec=grid_spec,
        out_shape=jax.ShapeDtypeStruct((n_win * stage, half), U32),
        compiler_params=pltpu.CompilerParams(
            dimension_semantics=("arbitrary",), vmem_limit_bytes=VMEM_LIMIT_BYTES),
        name="experts",
    )(*tables, xs.reshape(n_win * stage, xw), w1, b1.reshape(depth, n_exp, 1, d_hid), w2,
      b2.reshape(depth, n_exp, 1, d))
    return ys.reshape(n_win, stage, half)


def _combine_kernel(x_ref, mod_ref, pos_ref, fng_ref, ys_ref, o_ref, *, final_norm):
    rows = x_ref.shape[0]
    stage = ys_ref.shape[0]
    pos_rows = jnp.concatenate(
        [pos_ref[...].astype(F32), jnp.full((LANES - TOP_K, rows), -1.0, F32)], axis=0)
    pos_t = jnp.transpose(pos_rows)
    c_iota = lax.broadcasted_iota(I32, (rows, stage), 1).astype(F32)
    hit = c_iota == pos_t[:, 0:1]
    for k in range(1, TOP_K):
        hit = hit | (c_iota == pos_t[:, k:k + 1])
    unperm = jnp.where(hit, 1.0, 0.0).astype(BF16)
    ya, yb = _unpack_bf16_pairs(ys_ref[...])
    moe = jnp.concatenate(
        [jnp.dot(unperm, ya.astype(BF16), preferred_element_type=F32),
         jnp.dot(unperm, yb.astype(BF16), preferred_element_type=F32)], axis=-1)
    out = x_ref[...] + mod_ref[5:6, :] * moe
    if final_norm:
        out = _rms_normalize(out) * fng_ref[...]
    o_ref[...] = out


def _combine(x1, mod_l, pos, ys, final_norm_g, final_norm):
    bsz, seq, d = x1.shape
    n_tok = bsz * seq
    rows = ROUTE_ROWS
    tiles_per_seq = seq // rows
    n_win, stage, half = ys.shape
    out = pl.pallas_call(
        functools.partial(_combine_kernel, final_norm=final_norm),
        grid=(n_win,),
        in_specs=[
            pl.BlockSpec((rows, d), lambda i: (i, 0)),
            pl.BlockSpec((None, N_MOD, d), lambda i: (i // tiles_per_seq, 0, 0)),
            pl.BlockSpec((TOP_K, rows), lambda i: (0, i)),
            pl.BlockSpec((1, d), lambda i: (0, 0)),
            pl.BlockSpec((None, stage, half), lambda i: (i, 0, 0)),
        ],
        out_specs=pl.BlockSpec((rows, d), lambda i: (i, 0)),
        out_shape=jax.ShapeDtypeStruct((n_tok, d), F32),
        compiler_params=pltpu.CompilerParams(
            dimension_semantics=("arbitrary",), vmem_limit_bytes=VMEM_LIMIT_BYTES),
        name="combine",
    )(x1.reshape(n_tok, d), mod_l, pos, final_norm_g.reshape(1, d), ys)
    return out.reshape(bsz, seq, d)


def _segment_tables(cnt, stage):
    n_win, n_exp = cnt.shape
    seg_len = _round_up(cnt, SEG_ALIGN)
    seg_src = jnp.cumsum(seg_len, axis=1) - seg_len
    used = jnp.sum(seg_len, axis=1)
    seg_off = jnp.cumsum(seg_len, axis=0) - seg_len
    reg_len = jnp.sum(seg_len, axis=0)
    reg_pad = _round_up(reg_len, EXPERT_ROWS)
    pad_end = jnp.cumsum(reg_pad)
    base = pad_end - reg_pad
    max_rows = n_win * (TOP_K * ROUTE_ROWS + n_exp * (SEG_ALIGN - 1)) + n_exp * (EXPERT_ROWS - SEG_ALIGN)
    n_blocks = -(-max_rows // EXPERT_ROWS)
    n_active = pad_end[-1] // EXPERT_ROWS
    gidx = jnp.arange(n_blocks, dtype=I32)
    blk_row = (gidx * EXPERT_ROWS)[:, None]
    in_region = (base[None, :] <= blk_row) & (blk_row < pad_end[None, :])
    valid = jnp.sum(jnp.where(in_region, jnp.clip((base + reg_len)[None, :] - blk_row, 0, EXPERT_ROWS), 0), axis=1)
    first = jnp.any(in_region & (base[None, :] == blk_row), axis=1)
    last_row = (jnp.minimum(gidx, n_active - 1) * EXPERT_ROWS)[:, None]
    block_e = jnp.minimum(jnp.sum(pad_end[None, :] <= last_row, axis=1), n_exp - 1)

    glob = (base[None, :] + seg_off).T.reshape(-1)
    seg_n = seg_len.T.reshape(-1)
    src = (jnp.arange(n_win, dtype=I32)[:, None] * stage + seg_src).T.reshape(-1)
    head_blk = glob // EXPERT_ROWS
    head_dst = glob % EXPERT_ROWS
    head_n = jnp.minimum(seg_n, EXPERT_ROWS - head_dst)
    carry_n = seg_n - head_n
    p_lo = jnp.sum(head_blk[None, :] < gidx[:, None], axis=1)
    p_hi = jnp.sum(head_blk[None, :] <= gidx[:, None], axis=1)
    carried = (head_blk[None, :] + 1 == gidx[:, None]) & (carry_n[None, :] > 0)
    c_n = jnp.sum(jnp.where(carried, carry_n[None, :], 0), axis=1)
    c_src = jnp.sum(jnp.where(carried, (src + head_n)[None, :], 0), axis=1)
    p_src, p_dst, p_n = (jnp.pad(t, (0, INLINE_PIECES + PIECE_UNROLL)) for t in (src, head_dst, head_n))
    tables = (block_e, first, n_active.reshape(1), valid, p_lo, p_hi, p_src, p_dst, p_n, c_src, c_n, used)
    return tuple(t.astype(I32) for t in tables)


def _moe_layer(x1, mod_l, norm2_g, router_w, router_b, layer, w1, b1, w2, b2, final_norm_g, final_norm):
    xs, pos, cnt = _router(x1, mod_l, norm2_g, router_w, router_b)
    tables = _segment_tables(cnt[:, :, 0], xs.shape[1])
    ys = _experts(xs, tables, layer, w1, b1, w2, b2)
    return _combine(x1, mod_l, pos, ys, final_norm_g, final_norm)


def kernel(x, c, norm1_g, norm2_g, w_ada, b_ada, w_in, conv_a_w, sgu_w, sgu_b, sgu_ln_g, sgu_ln_b,
           conv_c_w, conv_c_b, rglru_gate_w, rglru_gate_b, rglru_lambda, mix_norm_g, w_out, router_w,
           router_b, expert_w1, expert_b1, expert_w2, expert_b2, final_norm_g):
    depth = w_ada.shape[0]
    mod = _adaln_modulation(c, w_ada, b_ada)
    for l in range(depth):
        x1 = _mixer(x, mod[l], norm1_g[l], w_in[l], conv_a_w[l], sgu_w[l], sgu_b[l], sgu_ln_g[l],
                    sgu_ln_b[l], conv_c_w[l], conv_c_b[l], rglru_gate_w[l], rglru_gate_b[l],
                    rglru_lambda[l], mix_norm_g[l], w_out[l])
        x = _moe_layer(x1, mod[l], norm2_g[l], router_w[l], router_b[l], l, expert_w1, expert_b1,
                       expert_w2, expert_b2, final_norm_g, l == depth - 1)
    return x
```

```python
import functools

import jax
import jax.numpy as jnp
from jax import lax
from jax.experimental import pallas as pl
from jax.experimental.pallas import tpu as pltpu

F32, BF16, I32, U32 = jnp.float32, jnp.bfloat16, jnp.int32, jnp.uint32

HEAD_DIM = 64
CHUNK = 128
RGLRU_C = 8.0
TOP_K = 4
SWIGLU_LIMIT = 7.0
SWIGLU_ALPHA = 1.702
N_MOD = 6
EPS = 1e-6

SUBLANES = 8
LANES = 128
VMEM_LIMIT_BYTES = 56 * 1024 * 1024

MIX_ROWS = 512
ROUTE_ROWS = 256
ROUTE_STEP_WINDOWS = 2
PIECE_UNROLL = 4
INLINE_PIECES = 20
EXPERT_ROWS = 512
SEG_ALIGN = SUBLANES
MOD_COLS = 1024


def _rms_normalize(y):
    return y * lax.rsqrt(jnp.mean(y * y, axis=-1, keepdims=True) + EPS)


def _round_up(x, m):
    return (x + m - 1) // m * m


def _mod_kernel(c_ref, w_ref, b_ref, o_ref):
    o_ref[...] = (
        jnp.dot(c_ref[...], w_ref[...], preferred_element_type=F32, precision=lax.Precision.HIGHEST)
        + b_ref[...]
    )


def _adaln_modulation(c, w_ada, b_ada):
    depth, d, n = w_ada.shape
    bsz = c.shape[0]
    rows = -(-bsz // SUBLANES) * SUBLANES
    c_pad = jnp.pad(c, ((0, rows - bsz), (0, 0)))
    out = pl.pallas_call(
        _mod_kernel,
        grid=(depth, n // MOD_COLS),
        in_specs=[
            pl.BlockSpec((rows, d), lambda l, j: (0, 0)),
            pl.BlockSpec((None, d, MOD_COLS), lambda l, j: (l, 0, j)),
            pl.BlockSpec((None, 1, MOD_COLS), lambda l, j: (l, 0, j)),
        ],
        out_specs=pl.BlockSpec((None, rows, MOD_COLS), lambda l, j: (l, 0, j)),
        out_shape=jax.ShapeDtypeStruct((depth, rows, n), F32),
        name="adaln_mod",
    )(c_pad, w_ada, b_ada.reshape(depth, 1, n))
    return out[:, :bsz].reshape(depth, bsz, N_MOD, d)


def _mixer_kernel(x_ref, mod_ref, n1g_ref, win_ref, caw_ref, sguw_ref, sgub_ref, lng_ref, lnb_ref,
                  ccw_ref, ccb_ref, gw_ref, gb_ref, lam_ref, mng_ref, wout_ref,
                  o_ref, cvh_ref, zxh_ref, hc_ref, a_scr, b_scr, *, w_conv, w_sgu, w_lru):
    rows = x_ref.shape[0]
    hist = SUBLANES

    @pl.when(pl.program_id(1) == 0)
    def _():
        cvh_ref[0:hist, :] = jnp.zeros((hist, w_conv), F32)
        zxh_ref[0:hist, :] = jnp.zeros((hist, w_lru), F32)
        hc_ref[...] = jnp.zeros_like(hc_ref)

    x = x_ref[...]
    shift1, scale1, gate1 = mod_ref[0:1, :], mod_ref[1:2, :], mod_ref[2:3, :]
    h = _rms_normalize(x) * n1g_ref[...]
    h = h * (1.0 + scale1) + shift1
    z = jnp.dot(h.astype(BF16), win_ref[...], preferred_element_type=F32)

    o = 0
    z_b = z[:, o:o + w_conv]; o += w_conv
    z_c = z[:, o:o + w_conv]; o += w_conv
    z_v = z[:, o:o + w_conv]; o += w_conv
    z_su = z[:, o:o + w_sgu]; o += w_sgu
    z_sv = z[:, o:o + w_sgu]; o += w_sgu
    z_g = z[:, o:o + w_lru]; o += w_lru
    z_x = z[:, o:o + w_lru]

    cv = z_c * z_v
    cvh_ref[hist:hist + rows, :] = cv
    conv = (caw_ref[0:1, :] * cvh_ref[hist - 2:hist - 2 + rows, :]
            + caw_ref[1:2, :] * cvh_ref[hist - 1:hist - 1 + rows, :]
            + caw_ref[2:3, :] * cv)
    cvh_ref[0:hist, :] = cvh_ref[rows:rows + hist, :]
    y_conv = z_b * conv

    u = jax.nn.gelu(z_su)
    v = jax.nn.gelu(z_sv)
    vc = v - jnp.mean(v, axis=-1, keepdims=True)
    vn = vc * lax.rsqrt(jnp.mean(vc * vc, axis=-1, keepdims=True) + EPS)
    vn = (vn * lng_ref[...] + lnb_ref[...]).astype(BF16)
    n_heads = w_sgu // HEAD_DIM
    wr = lax.broadcasted_iota(I32, (n_heads * CHUNK, CHUNK), 0)
    wc = lax.broadcasted_iota(I32, (n_heads * CHUNK, CHUNK), 1)
    w_causal = jnp.where(wc <= (wr & (CHUNK - 1)), sguw_ref[...], 0.0).astype(BF16)
    lane = lax.broadcasted_iota(I32, (CHUNK, w_sgu), 1)
    mixed = []
    for ci in range(rows // CHUNK):
        res = jnp.dot(w_causal, vn[ci * CHUNK:(ci + 1) * CHUNK, :], preferred_element_type=F32)
        m = res[0:CHUNK, :]
        for hh in range(1, n_heads):
            m = jnp.where(lane >= hh * HEAD_DIM, res[hh * CHUNK:(hh + 1) * CHUNK, :], m)
        mixed.append(m + sgub_ref[...])
    y_sgu = u * jnp.concatenate(mixed, axis=0)

    zxh_ref[hist:hist + rows, :] = z_x
    xr = (ccw_ref[0:1, :] * zxh_ref[hist - 3:hist - 3 + rows, :]
          + ccw_ref[1:2, :] * zxh_ref[hist - 2:hist - 2 + rows, :]
          + ccw_ref[2:3, :] * zxh_ref[hist - 1:hist - 1 + rows, :]
          + ccw_ref[3:4, :] * z_x + ccb_ref[...])
    zxh_ref[0:hist, :] = zxh_ref[rows:rows + hist, :]
    xr_b = xr.astype(BF16)
    half = w_lru // 2
    g0 = jnp.dot(xr_b[:, :half], gw_ref[0], preferred_element_type=F32)
    g1 = jnp.dot(xr_b[:, half:], gw_ref[1], preferred_element_type=F32)
    r_gate = jax.nn.sigmoid(jnp.concatenate([g0[:, :half], g1[:, :half]], axis=-1) + gb_ref[0:1, :])
    i_gate = jax.nn.sigmoid(jnp.concatenate([g0[:, half:], g1[:, half:]], axis=-1) + gb_ref[1:2, :])
    neg_lam = -lam_ref[...]
    softplus = jnp.maximum(neg_lam, 0.0) + jnp.log1p(jnp.exp(-jnp.abs(neg_lam)))
    log_a = (-RGLRU_C) * r_gate * softplus
    a = jnp.exp(log_a)
    a_scr[...] = a
    one_m_a2 = 1.0 - a * a
    b_scr[...] = xr * i_gate * jnp.where(one_m_a2 > 0.0, one_m_a2 * lax.rsqrt(one_m_a2), 0.0)

    row8 = lax.broadcasted_iota(I32, (SUBLANES, w_lru), 0)

    def scan_group(g, carry):
        r0 = pl.multiple_of(g * SUBLANES, SUBLANES)
        a8 = a_scr[pl.ds(r0, SUBLANES), :]
        b8 = b_scr[pl.ds(r0, SUBLANES), :]
        for d in (1, 2, 4):
            keep = row8 >= d
            b8 = jnp.where(keep, a8 * pltpu.roll(b8, d, 0) + b8, b8)
            a8 = jnp.where(keep, a8 * pltpu.roll(a8, d, 0), a8)
        h8 = a8 * carry + b8
        b_scr[pl.ds(r0, SUBLANES), :] = h8
        return jnp.broadcast_to(h8[SUBLANES - 1:SUBLANES, :], (SUBLANES, w_lru))

    hc_ref[...] = lax.fori_loop(0, rows // SUBLANES, scan_group, hc_ref[...], unroll=True)
    y_lru = jax.nn.gelu(z_g) * b_scr[...]

    y = jnp.concatenate([_rms_normalize(y_conv), _rms_normalize(y_sgu), _rms_normalize(y_lru)], axis=-1)
    y = (y * mng_ref[...]).astype(BF16)
    o_ref[...] = x + gate1 * jnp.dot(y, wout_ref[...], preferred_element_type=F32)


def _mixer(x, mod_l, norm1_g, w_in, conv_a_w, sgu_w, sgu_b, sgu_ln_g, sgu_ln_b, conv_c_w, conv_c_b,
           rglru_gate_w, rglru_gate_b, rglru_lambda, mix_norm_g, w_out):
    bsz, seq, d = x.shape
    w_conv = conv_a_w.shape[1]
    w_sgu = sgu_ln_g.shape[0]
    w_lru = rglru_lambda.shape[0]
    d_in = w_in.shape[1]
    n_sgu_heads = sgu_w.shape[0]
    n_lru_heads = rglru_gate_w.shape[1]
    rows = MIX_ROWS
    half = w_lru // 2

    sgu_bias = jnp.repeat(sgu_b.T, HEAD_DIM, axis=1)
    eye = jnp.eye(n_lru_heads, dtype=F32)
    bd = jnp.einsum("ghij,hk->ghikj", rglru_gate_w, eye).reshape(2, w_lru, w_lru)
    gate_w = jnp.stack([
        jnp.concatenate([bd[0, j * half:(j + 1) * half, j * half:(j + 1) * half],
                         bd[1, j * half:(j + 1) * half, j * half:(j + 1) * half]], axis=1)
        for j in range(2)]).astype(BF16)

    full = lambda shape: pl.BlockSpec(shape, lambda b, s: (0,) * len(shape))
    kern = functools.partial(_mixer_kernel, w_conv=w_conv, w_sgu=w_sgu, w_lru=w_lru)
    return pl.pallas_call(
        kern,
        grid=(bsz, seq // rows),
        in_specs=[
            pl.BlockSpec((None, rows, d), lambda b, s: (b, s, 0)),
            pl.BlockSpec((None, N_MOD, d), lambda b, s: (b, 0, 0)),
            full((1, d)),
            full((d, d_in)),
            full(conv_a_w.shape),
            full((n_sgu_heads * CHUNK, CHUNK)),
            full((CHUNK, w_sgu)),
            full((1, w_sgu)),
            full((1, w_sgu)),
            full(conv_c_w.shape),
            full((1, w_lru)),
            full((2, half, 2 * half)),
            full((2, w_lru)),
            full((1, w_lru)),
            full((1, d)),
            full((d, d)),
        ],
        out_specs=pl.BlockSpec((None, rows, d), lambda b, s: (b, s, 0)),
        out_shape=jax.ShapeDtypeStruct((bsz, seq, d), F32),
        scratch_shapes=[
            pltpu.VMEM((rows + SUBLANES, w_conv), F32),
            pltpu.VMEM((rows + SUBLANES, w_lru), F32),
            pltpu.VMEM((SUBLANES, w_lru), F32),
            pltpu.VMEM((rows, w_lru), F32),
            pltpu.VMEM((rows, w_lru), F32),
        ],
        compiler_params=pltpu.CompilerParams(
            dimension_semantics=("arbitrary", "arbitrary"), vmem_limit_bytes=VMEM_LIMIT_BYTES),
        name="mixer",
    )(x, mod_l, norm1_g.reshape(1, d), w_in.astype(BF16), conv_a_w,
      sgu_w.reshape(n_sgu_heads * CHUNK, CHUNK), sgu_bias, sgu_ln_g.reshape(1, w_sgu),
      sgu_ln_b.reshape(1, w_sgu), conv_c_w, conv_c_b.reshape(1, w_lru), gate_w,
      rglru_gate_b.reshape(2, w_lru), rglru_lambda.reshape(1, w_lru), mix_norm_g.reshape(1, d),
      w_out.astype(BF16))


def _pack_bf16_pairs(a, b):
    ua = lax.bitcast_convert_type(a, U32)
    ub = lax.bitcast_convert_type(b, U32)
    return ua | (ub >> 16)


def _unpack_bf16_pairs(p):
    a = lax.bitcast_convert_type(p & jnp.uint32(0xFFFF0000), F32)
    b = lax.bitcast_convert_type(p << 16, F32)
    return a, b


def _stage_rows(n_exp):
    return _round_up(TOP_K * ROUTE_ROWS + n_exp * (SEG_ALIGN - 1), ROUTE_ROWS)


def _modulated_norm(x, g, scale, shift):
    return (_rms_normalize(x) * g) * (1.0 + scale) + shift


def _split_bf16(v):
    hi = v.astype(BF16)
    return hi, (v - hi.astype(F32)).astype(BF16)


def _router_kernel(x_ref, mod_ref, n2g_ref, rwt_ref, rb_ref, xs_ref, pos_ref, cnt_ref):
    for wi in range(xs_ref.shape[0]):
        sl = slice(wi * ROUTE_ROWS, (wi + 1) * ROUTE_ROWS)
        _route_window(x_ref[sl, :], mod_ref, n2g_ref, rwt_ref, rb_ref,
                      xs_ref.at[wi], pos_ref.at[:, sl], cnt_ref.at[wi])


def _route_window(x, mod_ref, n2g_ref, rwt_ref, rb_ref, xs_ref, pos_ref, cnt_ref):
    rows, d = x.shape
    n_exp = rwt_ref.shape[0]
    stage = xs_ref.shape[0]
    half = d // 2
    nt = (((1,), (1,)), ((), ()))

    h2 = _modulated_norm(x, n2g_ref[...], mod_ref[4:5, :], mod_ref[3:4, :])
    h_hi, h_lo = _split_bf16(h2)
    rw_hi, rw_lo = _split_bf16(rwt_ref[...])
    logits = (lax.dot_general(rw_hi, h_hi, nt, preferred_element_type=F32)
              + lax.dot_general(rw_hi, h_lo, nt, preferred_element_type=F32)
              + lax.dot_general(rw_lo, h_hi, nt, preferred_element_type=F32))
    logits = logits + rb_ref[...]

    e_iota = lax.broadcasted_iota(I32, (n_exp, rows), 0).astype(F32)
    work = logits
    tops, hots, idxs = [], [], []
    for k in range(TOP_K):
        m = jnp.max(work, axis=0, keepdims=True)
        idx = jnp.min(jnp.where(work == m, e_iota, float(n_exp)), axis=0, keepdims=True)
        hot = e_iota == idx
        work = jnp.where(hot, -jnp.inf, work)
        tops.append(m)
        hots.append(hot)
        idxs.append(idx)

    p = [jnp.exp(t - tops[0]) for t in tops]
    denom = p[0] + p[1] + p[2] + p[3]

    sel = jnp.zeros((n_exp, rows), F32)
    for hot in hots:
        sel = sel + hot.astype(F32)
    s_io = lax.broadcasted_iota(I32, (rows, rows), 0)
    t_io = lax.broadcasted_iota(I32, (rows, rows), 1)
    before = (s_io < t_io).astype(BF16)
    rank_e = jnp.dot(sel.astype(BF16), before, preferred_element_type=F32)
    n_e = jnp.sum(sel, axis=1, keepdims=True)
    seg_len = jnp.floor((n_e + (SEG_ALIGN - 1)) * (1.0 / SEG_ALIGN)) * SEG_ALIGN
    incl = jnp.broadcast_to(seg_len, (n_exp, LANES))
    e_row = lax.broadcasted_iota(I32, (n_exp, LANES), 0)
    step = 1
    while step < n_exp:
        incl = incl + jnp.where(e_row >= step, pltpu.roll(incl, step, 0), 0.0)
        step *= 2
    dest_e = (incl[:, 0:1] - seg_len) + rank_e

    r_iota = lax.broadcasted_iota(I32, (stage, rows), 0).astype(F32)
    onehot = None
    side = []
    for k in range(TOP_K):
        pos_k = jnp.sum(jnp.where(hots[k], dest_e, 0.0), axis=0, keepdims=True)
        pos_ref[k:k + 1, :] = pos_k.astype(I32)
        eq = r_iota == pos_k
        onehot = eq if onehot is None else (onehot | eq)
        side.append(p[k] / denom)
    perm = jnp.where(onehot, 1.0, 0.0).astype(BF16)

    w_rows = jnp.concatenate(side, axis=0)
    w_hi = w_rows.astype(BF16).astype(F32)
    side_rows = jnp.concatenate(
        [w_hi, w_rows - w_hi, jnp.concatenate(idxs, axis=0),
         jnp.zeros((LANES - 3 * TOP_K, rows), F32)], axis=0)
    side_cols = jnp.transpose(side_rows).astype(BF16)
    staged = jnp.dot(perm, jnp.concatenate([h_hi, side_cols], axis=1),
                     preferred_element_type=F32)
    xs_ref[:, 0:half] = _pack_bf16_pairs(staged[:, :half], staged[:, half:d])
    xs_ref[:, half:] = lax.bitcast_convert_type(staged[:, d:], U32)
    cnt_ref[...] = jnp.broadcast_to(n_e, (n_exp, LANES)).astype(I32)


def _router(x1, mod_l, norm2_g, router_w, router_b):
    bsz, seq, d = x1.shape
    n_tok = bsz * seq
    n_exp = router_w.shape[1]
    n_win = n_tok // ROUTE_ROWS
    wins = ROUTE_STEP_WINDOWS
    rows = wins * ROUTE_ROWS
    tiles_per_seq = seq // rows
    stage = _stage_rows(n_exp)
    return pl.pallas_call(
        _router_kernel,
        grid=(n_win // wins,),
        in_specs=[
            pl.BlockSpec((rows, d), lambda i: (i, 0)),
            pl.BlockSpec((None, N_MOD, d), lambda i: (i // tiles_per_seq, 0, 0)),
            pl.BlockSpec((1, d), lambda i: (0, 0)),
            pl.BlockSpec((n_exp, d), lambda i: (0, 0)),
            pl.BlockSpec((n_exp, 1), lambda i: (0, 0)),
        ],
        out_specs=[
            pl.BlockSpec((wins, stage, d // 2 + LANES), lambda i: (i, 0, 0)),
            pl.BlockSpec((TOP_K, rows), lambda i: (0, i)),
            pl.BlockSpec((wins, n_exp, LANES), lambda i: (i, 0, 0)),
        ],
        out_shape=[
            jax.ShapeDtypeStruct((n_win, stage, d // 2 + LANES), U32),
            jax.ShapeDtypeStruct((TOP_K, n_tok), I32),
            jax.ShapeDtypeStruct((n_win, n_exp, LANES), I32),
        ],
        compiler_params=pltpu.CompilerParams(
            dimension_semantics=("arbitrary",), vmem_limit_bytes=VMEM_LIMIT_BYTES),
        name="router",
    )(x1.reshape(n_tok, d), mod_l, norm2_g.reshape(1, d), router_w.T, router_b.reshape(n_exp, 1))


def _expert_kernel(be_sm, first_sm, nact_sm, valid_sm, plo_sm, phi_sm, psrc_sm, pdst_sm, pn_sm, csrc_sm,
                   cn_sm, used_sm, xs_ref, w1_ref, b1_ref, w2_ref, b2_ref, ys_ref,
                   w1b_ref, w2b_ref, xbuf, ybuf, zbuf, in_sem, out_sem, tail_sem, *, n_win, stage):
    g = pl.program_id(0)
    n_steps = pl.num_programs(0)
    nact = nact_sm[0]
    d_ff = w2_ref.shape[0]
    half = ybuf.shape[2]

    def pieces(blk, fn, enabled=True, inline=0):
        lo, hi = plo_sm[blk], phi_sm[blk]
        carried = cn_sm[blk]

        @pl.when(enabled & (carried > 0))
        def _():
            fn(pl.multiple_of(csrc_sm[blk], SEG_ALIGN), 0, pl.multiple_of(carried, SEG_ALIGN))

        def entry(p):
            n = pn_sm[p]

            @pl.when(enabled & (p < hi) & (n > 0))
            def _():
                fn(pl.multiple_of(psrc_sm[p], SEG_ALIGN), pl.multiple_of(pdst_sm[p], SEG_ALIGN),
                   pl.multiple_of(n, SEG_ALIGN))

        for j in range(inline):
            entry(lo + j)

        def body(i, c):
            for j in range(PIECE_UNROLL):
                entry(lo + inline + i * PIECE_UNROLL + j)
            return c
        left = jnp.maximum(hi - lo - inline, 0)
        trips = lax.shift_right_logical(left + (PIECE_UNROLL - 1), PIECE_UNROLL.bit_length() - 1)
        lax.fori_loop(0, jnp.where(enabled, trips, 0), body, 0)

    def in_copy(buf, src, dst, n):
        return pltpu.make_async_copy(
            xs_ref.at[pl.ds(src, n), :], xbuf.at[buf, pl.ds(dst, n), :], in_sem.at[buf])

    def out_copy(buf, src, dst, n):
        return pltpu.make_async_copy(
            ybuf.at[buf, pl.ds(dst, n), :], ys_ref.at[pl.ds(src, n), :], out_sem.at[buf])

    def valid_rows(blk):
        return pl.multiple_of(valid_sm[blk], SEG_ALIGN)

    def tail_copy(w):
        used = pl.multiple_of(used_sm[w], SEG_ALIGN)
        n = pl.multiple_of(stage - used, SEG_ALIGN)
        row = pl.multiple_of(w * stage + used, SEG_ALIGN)
        return pltpu.make_async_copy(zbuf.at[pl.ds(0, n), :], ys_ref.at[pl.ds(row, n), :], tail_sem)

    def for_windows(fn):
        def body(w, c):
            fn(w)
            return c
        lax.fori_loop(0, n_win, body, 0)

    @pl.when(g == 0)
    def _():
        xbuf[...] = jnp.zeros_like(xbuf)
        zbuf[...] = jnp.zeros_like(zbuf)
        for_windows(lambda w: tail_copy(w).start())
        pieces(0, lambda *a: in_copy(0, *a).start())

    @pl.when((g >= 2) & (g - 2 < nact))
    def _():
        out_copy(g % 2, 0, 0, valid_rows(g - 2)).wait()

    @pl.when(first_sm[g] == 1)
    def _():
        w1b_ref[...] = w1_ref[...].astype(BF16)
        w2b_ref[...] = w2_ref[...].astype(BF16)

    @pl.when(g < nact)
    def _():
        buf = g % 2
        pieces(g + 1, lambda *a: in_copy(1 - buf, *a).start(), enabled=g + 1 < nact, inline=INLINE_PIECES)
        in_copy(buf, 0, 0, valid_rows(g)).wait()
        packed = xbuf[buf]
        keep = lax.broadcasted_iota(I32, (EXPERT_ROWS, 1), 0) < valid_sm[g]
        xa, xb = _unpack_bf16_pairs(packed[:, :half])
        x = jnp.where(keep, jnp.concatenate([xa, xb], axis=-1), 0.0).astype(BF16)
        side = lax.bitcast_convert_type(packed[:, half:], F32)
        expert = be_sm[g].astype(F32)
        wcol = jnp.zeros((EXPERT_ROWS, 1), F32)
        for k in range(TOP_K):
            w_k = side[:, k:k + 1] + side[:, TOP_K + k:TOP_K + k + 1]
            wcol = jnp.where(side[:, 2 * TOP_K + k:2 * TOP_K + k + 1] == expert, w_k, wcol)
        wcol = jnp.where(keep, wcol, 0.0)
        hgu = jnp.dot(x, w1b_ref[...], preferred_element_type=F32) + b1_ref[...]
        gt = jnp.minimum(hgu[:, :d_ff], SWIGLU_LIMIT)
        up = jnp.clip(hgu[:, d_ff:], -SWIGLU_LIMIT, SWIGLU_LIMIT)
        act = (up + 1.0) * (gt * jax.nn.sigmoid(SWIGLU_ALPHA * gt))
        y = (jnp.dot(act.astype(BF16), w2b_ref[...], preferred_element_type=F32) + b2_ref[...]) * wcol
        y = y.astype(BF16).astype(F32)
        ybuf[buf] = _pack_bf16_pairs(y[:, :half], y[:, half:])
        pieces(g, lambda *a: out_copy(buf, *a).start(), inline=INLINE_PIECES)

    @pl.when(g == n_steps - 1)
    def _():
        for back in (1, 0):
            blk = g - back

            @pl.when((blk >= 0) & (blk < nact))
            def _():
                out_copy(blk % 2, 0, 0, valid_rows(blk)).wait()
        for_windows(lambda w: tail_copy(w).wait())


def _experts(xs, tables, layer, w1, b1, w2, b2):
    n_win, stage, xw = xs.shape
    depth, n_exp, d, d_hid = w1.shape
    d_ff = w2.shape[2]
    half = d // 2
    n_blocks = tables[0].shape[0]
    wmap = lambda g, be, *_: (layer, be[g], 0, 0)
    grid_spec = pltpu.PrefetchScalarGridSpec(
        num_scalar_prefetch=len(tables),
        grid=(n_blocks,),
        in_specs=[
            pl.BlockSpec(memory_space=pl.ANY),
            pl.BlockSpec((None, None, d, d_hid), wmap),
            pl.BlockSpec((None, None, 1, d_hid), wmap),
            pl.BlockSpec((None, None, d_ff, d), wmap),
            pl.BlockSpec((None, None, 1, d), wmap),
        ],
        out_specs=pl.BlockSpec(memory_space=pl.ANY),
        scratch_shapes=[
            pltpu.VMEM((d, d_hid), BF16),
            pltpu.VMEM((d_ff, d), BF16),
            pltpu.VMEM((2, EXPERT_ROWS, xw), U32),
            pltpu.VMEM((2, EXPERT_ROWS, half), U32),
            pltpu.VMEM((stage - TOP_K * ROUTE_ROWS, half), U32),
            pltpu.SemaphoreType.DMA((2,)),
            pltpu.SemaphoreType.DMA((2,)),
            pltpu.SemaphoreType.DMA,
        ],
    )
    ys = pl.pallas_call(
        functools.partial(_expert_kernel, n_win=n_win, stage=stage),
        grid_spec=grid_spec,
        out_shape=jax.ShapeDtypeStruct((n_win * stage, half), U32),
        compiler_params=pltpu.CompilerParams(
            dimension_semantics=("arbitrary",), vmem_limit_bytes=VMEM_LIMIT_BYTES),
        name="experts",
    )(*tables, xs.reshape(n_win * stage, xw), w1, b1.reshape(depth, n_exp, 1, d_hid), w2,
      b2.reshape(depth, n_exp, 1, d))
    return ys.reshape(n_win, stage, half)


def _combine_kernel(x_ref, mod_ref, pos_ref, fng_ref, ys_ref, o_ref, *, final_norm):
    rows = x_ref.shape[0]
    stage = ys_ref.shape[0]
    pos_rows = jnp.concatenate(
        [pos_ref[...].astype(F32), jnp.full((LANES - TOP_K, rows), -1.0, F32)], axis=0)
    pos_t = jnp.transpose(pos_rows)
    c_iota = lax.broadcasted_iota(I32, (rows, stage), 1).astype(F32)
    hit = c_iota == pos_t[:, 0:1]
    for k in range(1, TOP_K):
        hit = hit | (c_iota == pos_t[:, k:k + 1])
    unperm = jnp.where(hit, 1.0, 0.0).astype(BF16)
    ya, yb = _unpack_bf16_pairs(ys_ref[...])
    moe = jnp.concatenate(
        [jnp.dot(unperm, ya.astype(BF16), preferred_element_type=F32),
         jnp.dot(unperm, yb.astype(BF16), preferred_element_type=F32)], axis=-1)
    out = x_ref[...] + mod_ref[5:6, :] * moe
    if final_norm:
        out = _rms_normalize(out) * fng_ref[...]
    o_ref[...] = out


def _combine(x1, mod_l, pos, ys, final_norm_g, final_norm):
    bsz, seq, d = x1.shape
    n_tok = bsz * seq
    rows = ROUTE_ROWS
    tiles_per_seq = seq // rows
    n_win, stage, half = ys.shape
    out = pl.pallas_call(
        functools.partial(_combine_kernel, final_norm=final_norm),
        grid=(n_win,),
        in_specs=[
            pl.BlockSpec((rows, d), lambda i: (i, 0)),
            pl.BlockSpec((None, N_MOD, d), lambda i: (i // tiles_per_seq, 0, 0)),
            pl.BlockSpec((TOP_K, rows), lambda i: (0, i)),
            pl.BlockSpec((1, d), lambda i: (0, 0)),
            pl.BlockSpec((None, stage, half), lambda i: (i, 0, 0)),
        ],
        out_specs=pl.BlockSpec((rows, d), lambda i: (i, 0)),
        out_shape=jax.ShapeDtypeStruct((n_tok, d), F32),
        compiler_params=pltpu.CompilerParams(
            dimension_semantics=("arbitrary",), vmem_limit_bytes=VMEM_LIMIT_BYTES),
        name="combine",
    )(x1.reshape(n_tok, d), mod_l, pos, final_norm_g.reshape(1, d), ys)
    return out.reshape(bsz, seq, d)


def _segment_tables(cnt, stage):
    n_win, n_exp = cnt.shape
    seg_len = _round_up(cnt, SEG_ALIGN)
    seg_src = jnp.cumsum(seg_len, axis=1) - seg_len
    used = jnp.sum(seg_len, axis=1)
    seg_off = jnp.cumsum(seg_len, axis=0) - seg_len
    reg_len = jnp.sum(seg_len, axis=0)
    reg_pad = _round_up(reg_len, EXPERT_ROWS)
    pad_end = jnp.cumsum(reg_pad)
    base = pad_end - reg_pad
    max_rows = n_win * (TOP_K * ROUTE_ROWS + n_exp * (SEG_ALIGN - 1)) + n_exp * (EXPERT_ROWS - SEG_ALIGN)
    n_blocks = -(-max_rows // EXPERT_ROWS)
    n_active = pad_end[-1] // EXPERT_ROWS
    gidx = jnp.arange(n_blocks, dtype=I32)
    blk_row = (gidx * EXPERT_ROWS)[:, None]
    in_region = (base[None, :] <= blk_row) & (blk_row < pad_end[None, :])
    valid = jnp.sum(jnp.where(in_region, jnp.clip((base + reg_len)[None, :] - blk_row, 0, EXPERT_ROWS), 0), axis=1)
    first = jnp.any(in_region & (base[None, :] == blk_row), axis=1)
    last_row = (jnp.minimum(gidx, n_active - 1) * EXPERT_ROWS)[:, None]
    block_e = jnp.minimum(jnp.sum(pad_end[None, :] <= last_row, axis=1), n_exp - 1)

    glob = (base[None, :] + seg_off).T.reshape(-1)
    seg_n = seg_len.T.reshape(-1)
    src = (jnp.arange(n_win, dtype=I32)[:, None] * stage + seg_src).T.reshape(-1)
    head_blk = glob // EXPERT_ROWS
    head_dst = glob % EXPERT_ROWS
    head_n = jnp.minimum(seg_n, EXPERT_ROWS - head_dst)
    carry_n = seg_n - head_n
    p_lo = jnp.sum(head_blk[None, :] < gidx[:, None], axis=1)
    p_hi = jnp.sum(head_blk[None, :] <= gidx[:, None], axis=1)
    carried = (head_blk[None, :] + 1 == gidx[:, None]) & (carry_n[None, :] > 0)
    c_n = jnp.sum(jnp.where(carried, carry_n[None, :], 0), axis=1)
    c_src = jnp.sum(jnp.where(carried, (src + head_n)[None, :], 0), axis=1)
    p_src, p_dst, p_n = (jnp.pad(t, (0, INLINE_PIECES + PIECE_UNROLL)) for t in (src, head_dst, head_n))
    tables = (block_e, first, n_active.reshape(1), valid, p_lo, p_hi, p_src, p_dst, p_n, c_src, c_n, used)
    return tuple(t.astype(I32) for t in tables)


def _moe_layer(x1, mod_l, norm2_g, router_w, router_b, layer, w1, b1, w2, b2, final_norm_g, final_norm):
    xs, pos, cnt = _router(x1, mod_l, norm2_g, router_w, router_b)
    tables = _segment_tables(cnt[:, :, 0], xs.shape[1])
    ys = _experts(xs, tables, layer, w1, b1, w2, b2)
    return _combine(x1, mod_l, pos, ys, final_norm_g, final_norm)


def kernel(x, c, norm1_g, norm2_g, w_ada, b_ada, w_in, conv_a_w, sgu_w, sgu_b, sgu_ln_g, sgu_ln_b,
           conv_c_w, conv_c_b, rglru_gate_w, rglru_gate_b, rglru_lambda, mix_norm_g, w_out, router_w,
           router_b, expert_w1, expert_b1, expert_w2, expert_b2, final_norm_g):
    depth = w_ada.shape[0]
    mod = _adaln_modulation(c, w_ada, b_ada)
    for l in range(depth):
        x1 = _mixer(x, mod[l], norm1_g[l], w_in[l], conv_a_w[l], sgu_w[l], sgu_b[l], sgu_ln_g[l],
                    sgu_ln_b[l], conv_c_w[l], conv_c_b[l], rglru_gate_w[l], rglru_gate_b[l],
                    rglru_lambda[l], mix_norm_g[l], w_out[l])
        x = _moe_layer(x1, mod[l], norm2_g[l], router_w[l], router_b[l], l, expert_w1, expert_b1,
                       expert_w2, expert_b2, final_norm_g, l == depth - 1)
    return x
```

```python
import functools

import jax
import jax.numpy as jnp
from jax import lax
from jax.experimental import pallas as pl
from jax.experimental.pallas import tpu as pltpu

F32, BF16, I32, U32 = jnp.float32, jnp.bfloat16, jnp.int32, jnp.uint32

HEAD_DIM = 64
CHUNK = 128
RGLRU_C = 8.0
TOP_K = 4
SWIGLU_LIMIT = 7.0
SWIGLU_ALPHA = 1.702
N_MOD = 6
EPS = 1e-6

SUBLANES = 8
LANES = 128
VMEM_LIMIT_BYTES = 56 * 1024 * 1024

MIX_ROWS = 512
ROUTE_ROWS = 256
ROUTE_STEP_WINDOWS = 2
PIECE_UNROLL = 4
INLINE_PIECES = 20
EXPERT_ROWS = 512
SEG_ALIGN = SUBLANES
MOD_COLS = 1024


def _rms_normalize(y):
    return y * lax.rsqrt(jnp.mean(y * y, axis=-1, keepdims=True) + EPS)


def _round_up(x, m):
    return (x + m - 1) // m * m


def _mod_kernel(c_ref, w_ref, b_ref, o_ref):
    o_ref[...] = (
        jnp.dot(c_ref[...], w_ref[...], preferred_element_type=F32, precision=lax.Precision.HIGHEST)
        + b_ref[...]
    )


def _adaln_modulation(c, w_ada, b_ada):
    depth, d, n = w_ada.shape
    bsz = c.shape[0]
    rows = -(-bsz // SUBLANES) * SUBLANES
    c_pad = jnp.pad(c, ((0, rows - bsz), (0, 0)))
    out = pl.pallas_call(
        _mod_kernel,
        grid=(depth, n // MOD_COLS),
        in_specs=[
            pl.BlockSpec((rows, d), lambda l, j: (0, 0)),
            pl.BlockSpec((None, d, MOD_COLS), lambda l, j: (l, 0, j)),
            pl.BlockSpec((None, 1, MOD_COLS), lambda l, j: (l, 0, j)),
        ],
        out_specs=pl.BlockSpec((None, rows, MOD_COLS), lambda l, j: (l, 0, j)),
        out_shape=jax.ShapeDtypeStruct((depth, rows, n), F32),
        name="adaln_mod",
    )(c_pad, w_ada, b_ada.reshape(depth, 1, n))
    return out[:, :bsz].reshape(depth, bsz, N_MOD, d)


def _mixer_kernel(x_ref, mod_ref, n1g_ref, win_ref, caw_ref, sguw_ref, sgub_ref, lng_ref, lnb_ref,
                  ccw_ref, ccb_ref, gw_ref, gb_ref, lam_ref, mng_ref, wout_ref,
                  o_ref, cvh_ref, zxh_ref, hc_ref, a_scr, b_scr, *, w_conv, w_sgu, w_lru):
    rows = x_ref.shape[0]
    hist = SUBLANES

    @pl.when(pl.program_id(1) == 0)
    def _():
        cvh_ref[0:hist, :] = jnp.zeros((hist, w_conv), F32)
        zxh_ref[0:hist, :] = jnp.zeros((hist, w_lru), F32)
        hc_ref[...] = jnp.zeros_like(hc_ref)

    x = x_ref[...]
    shift1, scale1, gate1 = mod_ref[0:1, :], mod_ref[1:2, :], mod_ref[2:3, :]
    h = _rms_normalize(x) * n1g_ref[...]
    h = h * (1.0 + scale1) + shift1
    z = jnp.dot(h.astype(BF16), win_ref[...], preferred_element_type=F32)

    o = 0
    z_b = z[:, o:o + w_conv]; o += w_conv
    z_c = z[:, o:o + w_conv]; o += w_conv
    z_v = z[:, o:o + w_conv]; o += w_conv
    z_su = z[:, o:o + w_sgu]; o += w_sgu
    z_sv = z[:, o:o + w_sgu]; o += w_sgu
    z_g = z[:, o:o + w_lru]; o += w_lru
    z_x = z[:, o:o + w_lru]

    cv = z_c * z_v
    cvh_ref[hist:hist + rows, :] = cv
    conv = (caw_ref[0:1, :] * cvh_ref[hist - 2:hist - 2 + rows, :]
            + caw_ref[1:2, :] * cvh_ref[hist - 1:hist - 1 + rows, :]
            + caw_ref[2:3, :] * cv)
    cvh_ref[0:hist, :] = cvh_ref[rows:rows + hist, :]
    y_conv = z_b * conv

    u = jax.nn.gelu(z_su)
    v = jax.nn.gelu(z_sv)
    vc = v - jnp.mean(v, axis=-1, keepdims=True)
    vn = vc * lax.rsqrt(jnp.mean(vc * vc, axis=-1, keepdims=True) + EPS)
    vn = (vn * lng_ref[...] + lnb_ref[...]).astype(BF16)
    n_heads = w_sgu // HEAD_DIM
    wr = lax.broadcasted_iota(I32, (n_heads * CHUNK, CHUNK), 0)
    wc = lax.broadcasted_iota(I32, (n_heads * CHUNK, CHUNK), 1)
    w_causal = jnp.where(wc <= (wr & (CHUNK - 1)), sguw_ref[...], 0.0).astype(BF16)
    lane = lax.broadcasted_iota(I32, (CHUNK, w_sgu), 1)
    mixed = []
    for ci in range(rows // CHUNK):
        res = jnp.dot(w_causal, vn[ci * CHUNK:(ci + 1) * CHUNK, :], preferred_element_type=F32)
        m = res[0:CHUNK, :]
        for hh in range(1, n_heads):
            m = jnp.where(lane >= hh * HEAD_DIM, res[hh * CHUNK:(hh + 1) * CHUNK, :], m)
        mixed.append(m + sgub_ref[...])
    y_sgu = u * jnp.concatenate(mixed, axis=0)

    zxh_ref[hist:hist + rows, :] = z_x
    xr = (ccw_ref[0:1, :] * zxh_ref[hist - 3:hist - 3 + rows, :]
          + ccw_ref[1:2, :] * zxh_ref[hist - 2:hist - 2 + rows, :]
          + ccw_ref[2:3, :] * zxh_ref[hist - 1:hist - 1 + rows, :]
          + ccw_ref[3:4, :] * z_x + ccb_ref[...])
    zxh_ref[0:hist, :] = zxh_ref[rows:rows + hist, :]
    xr_b = xr.astype(BF16)
    half = w_lru // 2
    g0 = jnp.dot(xr_b[:, :half], gw_ref[0], preferred_element_type=F32)
    g1 = jnp.dot(xr_b[:, half:], gw_ref[1], preferred_element_type=F32)
    r_gate = jax.nn.sigmoid(jnp.concatenate([g0[:, :half], g1[:, :half]], axis=-1) + gb_ref[0:1, :])
    i_gate = jax.nn.sigmoid(jnp.concatenate([g0[:, half:], g1[:, half:]], axis=-1) + gb_ref[1:2, :])
    neg_lam = -lam_ref[...]
    softplus = jnp.maximum(neg_lam, 0.0) + jnp.log1p(jnp.exp(-jnp.abs(neg_lam)))
    log_a = (-RGLRU_C) * r_gate * softplus
    a = jnp.exp(log_a)
    a_scr[...] = a
    one_m_a2 = 1.0 - a * a
    b_scr[...] = xr * i_gate * jnp.where(one_m_a2 > 0.0, one_m_a2 * lax.rsqrt(one_m_a2), 0.0)

    row8 = lax.broadcasted_iota(I32, (SUBLANES, w_lru), 0)

    def scan_group(g, carry):
        r0 = pl.multiple_of(g * SUBLANES, SUBLANES)
        a8 = a_scr[pl.ds(r0, SUBLANES), :]
        b8 = b_scr[pl.ds(r0, SUBLANES), :]
        for d in (1, 2, 4):
            keep = row8 >= d
            b8 = jnp.where(keep, a8 * pltpu.roll(b8, d, 0) + b8, b8)
            a8 = jnp.where(keep, a8 * pltpu.roll(a8, d, 0), a8)
        h8 = a8 * carry + b8
        b_scr[pl.ds(r0, SUBLANES), :] = h8
        return jnp.broadcast_to(h8[SUBLANES - 1:SUBLANES, :], (SUBLANES, w_lru))

    hc_ref[...] = lax.fori_loop(0, rows // SUBLANES, scan_group, hc_ref[...], unroll=True)
    y_lru = jax.nn.gelu(z_g) * b_scr[...]

    y = jnp.concatenate([_rms_normalize(y_conv), _rms_normalize(y_sgu), _rms_normalize(y_lru)], axis=-1)
    y = (y * mng_ref[...]).astype(BF16)
    o_ref[...] = x + gate1 * jnp.dot(y, wout_ref[...], preferred_element_type=F32)


def _mixer(x, mod_l, norm1_g, w_in, conv_a_w, sgu_w, sgu_b, sgu_ln_g, sgu_ln_b, conv_c_w, conv_c_b,
           rglru_gate_w, rglru_gate_b, rglru_lambda, mix_norm_g, w_out):
    bsz, seq, d = x.shape
    w_conv = conv_a_w.shape[1]
    w_sgu = sgu_ln_g.shape[0]
    w_lru = rglru_lambda.shape[0]
    d_in = w_in.shape[1]
    n_sgu_heads = sgu_w.shape[0]
    n_lru_heads = rglru_gate_w.shape[1]
    rows = MIX_ROWS
    half = w_lru // 2

    sgu_bias = jnp.repeat(sgu_b.T, HEAD_DIM, axis=1)
    eye = jnp.eye(n_lru_heads, dtype=F32)
    bd = jnp.einsum("ghij,hk->ghikj", rglru_gate_w, eye).reshape(2, w_lru, w_lru)
    gate_w = jnp.stack([
        jnp.concatenate([bd[0, j * half:(j + 1) * half, j * half:(j + 1) * half],
                         bd[1, j * half:(j + 1) * half, j * half:(j + 1) * half]], axis=1)
        for j in range(2)]).astype(BF16)

    full = lambda shape: pl.BlockSpec(shape, lambda b, s: (0,) * len(shape))
    kern = functools.partial(_mixer_kernel, w_conv=w_conv, w_sgu=w_sgu, w_lru=w_lru)
    return pl.pallas_call(
        kern,
        grid=(bsz, seq // rows),
        in_specs=[
            pl.BlockSpec((None, rows, d), lambda b, s: (b, s, 0)),
            pl.BlockSpec((None, N_MOD, d), lambda b, s: (b, 0, 0)),
            full((1, d)),
            full((d, d_in)),
            full(conv_a_w.shape),
            full((n_sgu_heads * CHUNK, CHUNK)),
            full((CHUNK, w_sgu)),
            full((1, w_sgu)),
            full((1, w_sgu)),
            full(conv_c_w.shape),
            full((1, w_lru)),
            full((2, half, 2 * half)),
            full((2, w_lru)),
            full((1, w_lru)),
            full((1, d)),
            full((d, d)),
        ],
        out_specs=pl.BlockSpec((None, rows, d), lambda b, s: (b, s, 0)),
        out_shape=jax.ShapeDtypeStruct((bsz, seq, d), F32),
        scratch_shapes=[
            pltpu.VMEM((rows + SUBLANES, w_conv), F32),
            pltpu.VMEM((rows + SUBLANES, w_lru), F32),
            pltpu.VMEM((SUBLANES, w_lru), F32),
            pltpu.VMEM((rows, w_lru), F32),
            pltpu.VMEM((rows, w_lru), F32),
        ],
        compiler_params=pltpu.CompilerParams(
            dimension_semantics=("arbitrary", "arbitrary"), vmem_limit_bytes=VMEM_LIMIT_BYTES),
        name="mixer",
    )(x, mod_l, norm1_g.reshape(1, d), w_in.astype(BF16), conv_a_w,
      sgu_w.reshape(n_sgu_heads * CHUNK, CHUNK), sgu_bias, sgu_ln_g.reshape(1, w_sgu),
      sgu_ln_b.reshape(1, w_sgu), conv_c_w, conv_c_b.reshape(1, w_lru), gate_w,
      rglru_gate_b.reshape(2, w_lru), rglru_lambda.reshape(1, w_lru), mix_norm_g.reshape(1, d),
      w_out.astype(BF16))


def _pack_bf16_pairs(a, b):
    ua = lax.bitcast_convert_type(a, U32)
    ub = lax.bitcast_convert_type(b, U32)
    return ua | (ub >> 16)


def _unpack_bf16_pairs(p):
    a = lax.bitcast_convert_type(p & jnp.uint32(0xFFFF0000), F32)
    b = lax.bitcast_convert_type(p << 16, F32)
    return a, b


def _stage_rows(n_exp):
    return _round_up(TOP_K * ROUTE_ROWS + n_exp * (SEG_ALIGN - 1), ROUTE_ROWS)


def _modulated_norm(x, g, scale, shift):
    return (_rms_normalize(x) * g) * (1.0 + scale) + shift


def _split_bf16(v):
    hi = v.astype(BF16)
    return hi, (v - hi.astype(F32)).astype(BF16)


def _router_kernel(x_ref, mod_ref, n2g_ref, rwt_ref, rb_ref, xs_ref, pos_ref, cnt_ref):
    for wi in range(xs_ref.shape[0]):
        sl = slice(wi * ROUTE_ROWS, (wi + 1) * ROUTE_ROWS)
        _route_window(x_ref[sl, :], mod_ref, n2g_ref, rwt_ref, rb_ref,
                      xs_ref.at[wi], pos_ref.at[:, sl], cnt_ref.at[wi])


def _route_window(x, mod_ref, n2g_ref, rwt_ref, rb_ref, xs_ref, pos_ref, cnt_ref):
    rows, d = x.shape
    n_exp = rwt_ref.shape[0]
    stage = xs_ref.shape[0]
    half = d // 2
    nt = (((1,), (1,)), ((), ()))

    h2 = _modulated_norm(x, n2g_ref[...], mod_ref[4:5, :], mod_ref[3:4, :])
    h_hi, h_lo = _split_bf16(h2)
    rw_hi, rw_lo = _split_bf16(rwt_ref[...])
    logits = (lax.dot_general(rw_hi, h_hi, nt, preferred_element_type=F32)
              + lax.dot_general(rw_hi, h_lo, nt, preferred_element_type=F32)
              + lax.dot_general(rw_lo, h_hi, nt, preferred_element_type=F32))
    logits = logits + rb_ref[...]

    e_iota = lax.broadcasted_iota(I32, (n_exp, rows), 0).astype(F32)
    work = logits
    tops, hots, idxs = [], [], []
    for k in range(TOP_K):
        m = jnp.max(work, axis=0, keepdims=True)
        idx = jnp.min(jnp.where(work == m, e_iota, float(n_exp)), axis=0, keepdims=True)
        hot = e_iota == idx
        work = jnp.where(hot, -jnp.inf, work)
        tops.append(m)
        hots.append(hot)
        idxs.append(idx)

    p = [jnp.exp(t - tops[0]) for t in tops]
    denom = p[0] + p[1] + p[2] + p[3]

    sel = jnp.zeros((n_exp, rows), F32)
    for hot in hots:
        sel = sel + hot.astype(F32)
    s_io = lax.broadcasted_iota(I32, (rows, rows), 0)
    t_io = lax.broadcasted_iota(I32, (rows, rows), 1)
    before = (s_io < t_io).astype(BF16)
    rank_e = jnp.dot(sel.astype(BF16), before, preferred_element_type=F32)
    n_e = jnp.sum(sel, axis=1, keepdims=True)
    seg_len = jnp.floor((n_e + (SEG_ALIGN - 1)) * (1.0 / SEG_ALIGN)) * SEG_ALIGN
    incl = jnp.broadcast_to(seg_len, (n_exp, LANES))
    e_row = lax.broadcasted_iota(I32, (n_exp, LANES), 0)
    step = 1
    while step < n_exp:
        incl = incl + jnp.where(e_row >= step, pltpu.roll(incl, step, 0), 0.0)
        step *= 2
    dest_e = (incl[:, 0:1] - seg_len) + rank_e

    r_iota = lax.broadcasted_iota(I32, (stage, rows), 0).astype(F32)
    onehot = None
    side = []
    for k in range(TOP_K):
        pos_k = jnp.sum(jnp.where(hots[k], dest_e, 0.0), axis=0, keepdims=True)
        pos_ref[k:k + 1, :] = pos_k.astype(I32)
        eq = r_iota == pos_k
        onehot = eq if onehot is None else (onehot | eq)
        side.append(p[k] / denom)
    perm = jnp.where(onehot, 1.0, 0.0).astype(BF16)

    w_rows = jnp.concatenate(side, axis=0)
    w_hi = w_rows.astype(BF16).astype(F32)
    side_rows = jnp.concatenate(
        [w_hi, w_rows - w_hi, jnp.concatenate(idxs, axis=0),
         jnp.zeros((LANES - 3 * TOP_K, rows), F32)], axis=0)
    side_cols = jnp.transpose(side_rows).astype(BF16)
    staged = jnp.dot(perm, jnp.concatenate([h_hi, side_cols], axis=1),
                     preferred_element_type=F32)
    xs_ref[:, 0:half] = _pack_bf16_pairs(staged[:, :half], staged[:, half:d])
    xs_ref[:, half:] = lax.bitcast_convert_type(staged[:, d:], U32)
    cnt_ref[...] = jnp.broadcast_to(n_e, (n_exp, LANES)).astype(I32)


def _router(x1, mod_l, norm2_g, router_w, router_b):
    bsz, seq, d = x1.shape
    n_tok = bsz * seq
    n_exp = router_w.shape[1]
    n_win = n_tok // ROUTE_ROWS
    wins = ROUTE_STEP_WINDOWS
    rows = wins * ROUTE_ROWS
    tiles_per_seq = seq // rows
    stage = _stage_rows(n_exp)
    return pl.pallas_call(
        _router_kernel,
        grid=(n_win // wins,),
        in_specs=[
            pl.BlockSpec((rows, d), lambda i: (i, 0)),
            pl.BlockSpec((None, N_MOD, d), lambda i: (i // tiles_per_seq, 0, 0)),
            pl.BlockSpec((1, d), lambda i: (0, 0)),
            pl.BlockSpec((n_exp, d), lambda i: (0, 0)),
            pl.BlockSpec((n_exp, 1), lambda i: (0, 0)),
        ],
        out_specs=[
            pl.BlockSpec((wins, stage, d // 2 + LANES), lambda i: (i, 0, 0)),
            pl.BlockSpec((TOP_K, rows), lambda i: (0, i)),
            pl.BlockSpec((wins, n_exp, LANES), lambda i: (i, 0, 0)),
        ],
        out_shape=[
            jax.ShapeDtypeStruct((n_win, stage, d // 2 + LANES), U32),
            jax.ShapeDtypeStruct((TOP_K, n_tok), I32),
            jax.ShapeDtypeStruct((n_win, n_exp, LANES), I32),
        ],
        compiler_params=pltpu.CompilerParams(
            dimension_semantics=("arbitrary",), vmem_limit_bytes=VMEM_LIMIT_BYTES),
        name="router",
    )(x1.reshape(n_tok, d), mod_l, norm2_g.reshape(1, d), router_w.T, router_b.reshape(n_exp, 1))


def _expert_kernel(be_sm, first_sm, nact_sm, valid_sm, plo_sm, phi_sm, psrc_sm, pdst_sm, pn_sm, csrc_sm,
                   cn_sm, used_sm, xs_ref, w1_ref, b1_ref, w2_ref, b2_ref, ys_ref,
                   w1b_ref, w2b_ref, xbuf, ybuf, zbuf, in_sem, out_sem, tail_sem, *, n_win, stage):
    g = pl.program_id(0)
    n_steps = pl.num_programs(0)
    nact = nact_sm[0]
    d_ff = w2_ref.shape[0]
    half = ybuf.shape[2]

    def pieces(blk, fn, enabled=True, inline=0):
        lo, hi = plo_sm[blk], phi_sm[blk]

        def group(first):
            idx = [first + j for j in range(PIECE_UNROLL)]
            rows = [(p, pn_sm[p], psrc_sm[p], pdst_sm[p]) for p in idx]
            for p, n, src, dst in rows:
                @pl.when(enabled & (p < hi) & (n > 0))
                def _():
                    fn(pl.multiple_of(src, SEG_ALIGN), pl.multiple_of(dst, SEG_ALIGN),
                       pl.multiple_of(n, SEG_ALIGN))

        carried, carried_src = cn_sm[blk], csrc_sm[blk]

        @pl.when(enabled & (carried > 0))
        def _():
            fn(pl.multiple_of(carried_src, SEG_ALIGN), 0, pl.multiple_of(carried, SEG_ALIGN))

        for j in range(0, inline, PIECE_UNROLL):
            group(lo + j)

        def body(i, c):
            group(lo + inline + i * PIECE_UNROLL)
            return c
        left = jnp.maximum(hi - lo - inline, 0)
        trips = lax.shift_right_logical(left + (PIECE_UNROLL - 1), PIECE_UNROLL.bit_length() - 1)
        lax.fori_loop(0, jnp.where(enabled, trips, 0), body, 0)

    def in_copy(buf, src, dst, n):
        return pltpu.make_async_copy(
            xs_ref.at[pl.ds(src, n), :], xbuf.at[buf, pl.ds(dst, n), :], in_sem.at[buf])

    def out_copy(buf, src, dst, n):
        return pltpu.make_async_copy(
            ybuf.at[buf, pl.ds(dst, n), :], ys_ref.at[pl.ds(src, n), :], out_sem.at[buf])

    def valid_rows(blk):
        return pl.multiple_of(valid_sm[blk], SEG_ALIGN)

    def tail_copy(w):
        used = pl.multiple_of(used_sm[w], SEG_ALIGN)
        n = pl.multiple_of(stage - used, SEG_ALIGN)
        row = pl.multiple_of(w * stage + used, SEG_ALIGN)
        return pltpu.make_async_copy(zbuf.at[pl.ds(0, n), :], ys_ref.at[pl.ds(row, n), :], tail_sem)

    def for_windows(fn):
        def body(w, c):
            fn(w)
            return c
        lax.fori_loop(0, n_win, body, 0)

    @pl.when(g == 0)
    def _():
        xbuf[...] = jnp.zeros_like(xbuf)
        zbuf[...] = jnp.zeros_like(zbuf)
        for_windows(lambda w: tail_copy(w).start())
        pieces(0, lambda *a: in_copy(0, *a).start())

    @pl.when((g >= 2) & (g - 2 < nact))
    def _():
        out_copy(g % 2, 0, 0, valid_rows(g - 2)).wait()

    @pl.when(first_sm[g] == 1)
    def _():
        w1b_ref[...] = w1_ref[...].astype(BF16)
        w2b_ref[...] = w2_ref[...].astype(BF16)

    @pl.when(g < nact)
    def _():
        buf = g % 2
        pieces(g + 1, lambda *a: in_copy(1 - buf, *a).start(), enabled=g + 1 < nact, inline=INLINE_PIECES)
        in_copy(buf, 0, 0, valid_rows(g)).wait()
        packed = xbuf[buf]
        keep = lax.broadcasted_iota(I32, (EXPERT_ROWS, 1), 0) < valid_sm[g]
        xa, xb = _unpack_bf16_pairs(packed[:, :half])
        x = jnp.where(keep, jnp.concatenate([xa, xb], axis=-1), 0.0).astype(BF16)
        side = lax.bitcast_convert_type(packed[:, half:], F32)
        expert = be_sm[g].astype(F32)
        wcol = jnp.zeros((EXPERT_ROWS, 1), F32)
        for k in range(TOP_K):
            w_k = side[:, k:k + 1] + side[:, TOP_K + k:TOP_K + k + 1]
            wcol = jnp.where(side[:, 2 * TOP_K + k:2 * TOP_K + k + 1] == expert, w_k, wcol)
        wcol = jnp.where(keep, wcol, 0.0)
        hgu = jnp.dot(x, w1b_ref[...], preferred_element_type=F32) + b1_ref[...]
        gt = jnp.minimum(hgu[:, :d_ff], SWIGLU_LIMIT)
        up = jnp.clip(hgu[:, d_ff:], -SWIGLU_LIMIT, SWIGLU_LIMIT)
        act = (up + 1.0) * (gt * jax.nn.sigmoid(SWIGLU_ALPHA * gt))
        y = (jnp.dot(act.astype(BF16), w2b_ref[...], preferred_element_type=F32) + b2_ref[...]) * wcol
        y = y.astype(BF16).astype(F32)
        ybuf[buf] = _pack_bf16_pairs(y[:, :half], y[:, half:])
        pieces(g, lambda *a: out_copy(buf, *a).start(), inline=INLINE_PIECES)

    @pl.when(g == n_steps - 1)
    def _():
        for back in (1, 0):
            blk = g - back

            @pl.when((blk >= 0) & (blk < nact))
            def _():
                out_copy(blk % 2, 0, 0, valid_rows(blk)).wait()
        for_windows(lambda w: tail_copy(w).wait())


def _experts(xs, tables, layer, w1, b1, w2, b2):
    n_win, stage, xw = xs.shape
    depth, n_exp, d, d_hid = w1.shape
    d_ff = w2.shape[2]
    half = d // 2
    n_blocks = tables[0].shape[0]
    wmap = lambda g, be, *_: (layer, be[g], 0, 0)
    grid_spec = pltpu.PrefetchScalarGridSpec(
        num_scalar_prefetch=len(tables),
        grid=(n_blocks,),
        in_specs=[
            pl.BlockSpec(memory_space=pl.ANY),
            pl.BlockSpec((None, None, d, d_hid), wmap),
            pl.BlockSpec((None, None, 1, d_hid), wmap),
            pl.BlockSpec((None, None, d_ff, d), wmap),
            pl.BlockSpec((None, None, 1, d), wmap),
        ],
        out_specs=pl.BlockSpec(memory_space=pl.ANY),
        scratch_shapes=[
            pltpu.VMEM((d, d_hid), BF16),
            pltpu.VMEM((d_ff, d), BF16),
            pltpu.VMEM((2, EXPERT_ROWS, xw), U32),
            pltpu.VMEM((2, EXPERT_ROWS, half), U32),
            pltpu.VMEM((stage - TOP_K * ROUTE_ROWS, half), U32),
            pltpu.SemaphoreType.DMA((2,)),
            pltpu.SemaphoreType.DMA((2,)),
            pltpu.SemaphoreType.DMA,
        ],
    )
    ys = pl.pallas_call(
        functools.partial(_expert_kernel, n_win=n_win, stage=stage),
        grid_spec=grid_spec,
        out_shape=jax.ShapeDtypeStruct((n_win * stage, half), U32),
        compiler_params=pltpu.CompilerParams(
            dimension_semantics=("arbitrary",), vmem_limit_bytes=VMEM_LIMIT_BYTES),
        name="experts",
    )(*tables, xs.reshape(n_win * stage, xw), w1, b1.reshape(depth, n_exp, 1, d_hid), w2,
      b2.reshape(depth, n_exp, 1, d))
    return ys.reshape(n_win, stage, half)


def _combine_kernel(x_ref, mod_ref, pos_ref, fng_ref, ys_ref, o_ref, *, final_norm):
    rows = x_ref.shape[0]
    stage = ys_ref.shape[0]
    pos_rows = jnp.concatenate(
        [pos_ref[...].astype(F32), jnp.full((LANES - TOP_K, rows), -1.0, F32)], axis=0)
    pos_t = jnp.transpose(pos_rows)
    c_iota = lax.broadcasted_iota(I32, (rows, stage), 1).astype(F32)
    hit = c_iota == pos_t[:, 0:1]
    for k in range(1, TOP_K):
        hit = hit | (c_iota == pos_t[:, k:k + 1])
    unperm = jnp.where(hit, 1.0, 0.0).astype(BF16)
    ya, yb = _unpack_bf16_pairs(ys_ref[...])
    moe = jnp.concatenate(
        [jnp.dot(unperm, ya.astype(BF16), preferred_element_type=F32),
         jnp.dot(unperm, yb.astype(BF16), preferred_element_type=F32)], axis=-1)
    out = x_ref[...] + mod_ref[5:6, :] * moe
    if final_norm:
        out = _rms_normalize(out) * fng_ref[...]
    o_ref[...] = out


def _combine(x1, mod_l, pos, ys, final_norm_g, final_norm):
    bsz, seq, d = x1.shape
    n_tok = bsz * seq
    rows = ROUTE_ROWS
    tiles_per_seq = seq // rows
    n_win, stage, half = ys.shape
    out = pl.pallas_call(
        functools.partial(_combine_kernel, final_norm=final_norm),
        grid=(n_win,),
        in_specs=[
            pl.BlockSpec((rows, d), lambda i: (i, 0)),
            pl.BlockSpec((None, N_MOD, d), lambda i: (i // tiles_per_seq, 0, 0)),
            pl.BlockSpec((TOP_K, rows), lambda i: (0, i)),
            pl.BlockSpec((1, d), lambda i: (0, 0)),
            pl.BlockSpec((None, stage, half), lambda i: (i, 0, 0)),
        ],
        out_specs=pl.BlockSpec((rows, d), lambda i: (i, 0)),
        out_shape=jax.ShapeDtypeStruct((n_tok, d), F32),
        compiler_params=pltpu.CompilerParams(
            dimension_semantics=("arbitrary",), vmem_limit_bytes=VMEM_LIMIT_BYTES),
        name="combine",
    )(x1.reshape(n_tok, d), mod_l, pos, final_norm_g.reshape(1, d), ys)
    return out.reshape(bsz, seq, d)


def _segment_tables(cnt, stage):
    n_win, n_exp = cnt.shape
    seg_len = _round_up(cnt, SEG_ALIGN)
    seg_src = jnp.cumsum(seg_len, axis=1) - seg_len
    used = jnp.sum(seg_len, axis=1)
    seg_off = jnp.cumsum(seg_len, axis=0) - seg_len
    reg_len = jnp.sum(seg_len, axis=0)
    reg_pad = _round_up(reg_len, EXPERT_ROWS)
    pad_end = jnp.cumsum(reg_pad)
    base = pad_end - reg_pad
    max_rows = n_win * (TOP_K * ROUTE_ROWS + n_exp * (SEG_ALIGN - 1)) + n_exp * (EXPERT_ROWS - SEG_ALIGN)
    n_blocks = -(-max_rows // EXPERT_ROWS)
    n_active = pad_end[-1] // EXPERT_ROWS
    gidx = jnp.arange(n_blocks, dtype=I32)
    blk_row = (gidx * EXPERT_ROWS)[:, None]
    in_region = (base[None, :] <= blk_row) & (blk_row < pad_end[None, :])
    valid = jnp.sum(jnp.where(in_region, jnp.clip((base + reg_len)[None, :] - blk_row, 0, EXPERT_ROWS), 0), axis=1)
    first = jnp.any(in_region & (base[None, :] == blk_row), axis=1)
    last_row = (jnp.minimum(gidx, n_active - 1) * EXPERT_ROWS)[:, None]
    block_e = jnp.minimum(jnp.sum(pad_end[None, :] <= last_row, axis=1), n_exp - 1)

    glob = (base[None, :] + seg_off).T.reshape(-1)
    seg_n = seg_len.T.reshape(-1)
    src = (jnp.arange(n_win, dtype=I32)[:, None] * stage + seg_src).T.reshape(-1)
    head_blk = glob // EXPERT_ROWS
    head_dst = glob % EXPERT_ROWS
    head_n = jnp.minimum(seg_n, EXPERT_ROWS - head_dst)
    carry_n = seg_n - head_n
    p_lo = jnp.sum(head_blk[None, :] < gidx[:, None], axis=1)
    p_hi = jnp.sum(head_blk[None, :] <= gidx[:, None], axis=1)
    carried = (head_blk[None, :] + 1 == gidx[:, None]) & (carry_n[None, :] > 0)
    c_n = jnp.sum(jnp.where(carried, carry_n[None, :], 0), axis=1)
    c_src = jnp.sum(jnp.where(carried, (src + head_n)[None, :], 0), axis=1)
    p_src, p_dst, p_n = (jnp.pad(t, (0, INLINE_PIECES + PIECE_UNROLL)) for t in (src, head_dst, head_n))
    tables = (block_e, first, n_active.reshape(1), valid, p_lo, p_hi, p_src, p_dst, p_n, c_src, c_n, used)
    return tuple(t.astype(I32) for t in tables)


def _moe_layer(x1, mod_l, norm2_g, router_w, router_b, layer, w1, b1, w2, b2, final_norm_g, final_norm):
    xs, pos, cnt = _router(x1, mod_l, norm2_g, router_w, router_b)
    tables = _segment_tables(cnt[:, :, 0], xs.shape[1])
    ys = _experts(xs, tables, layer, w1, b1, w2, b2)
    return _combine(x1, mod_l, pos, ys, final_norm_g, final_norm)


def kernel(x, c, norm1_g, norm2_g, w_ada, b_ada, w_in, conv_a_w, sgu_w, sgu_b, sgu_ln_g, sgu_ln_b,
           conv_c_w, conv_c_b, rglru_gate_w, rglru_gate_b, rglru_lambda, mix_norm_g, w_out, router_w,
           router_b, expert_w1, expert_b1, expert_w2, expert_b2, final_norm_g):
    depth = w_ada.shape[0]
    mod = _adaln_modulation(c, w_ada, b_ada)
    for l in range(depth):
        x1 = _mixer(x, mod[l], norm1_g[l], w_in[l], conv_a_w[l], sgu_w[l], sgu_b[l], sgu_ln_g[l],
                    sgu_ln_b[l], conv_c_w[l], conv_c_b[l], rglru_gate_w[l], rglru_gate_b[l],
                    rglru_lambda[l], mix_norm_g[l], w_out[l])
        x = _moe_layer(x1, mod[l], norm2_g[l], router_w[l], router_b[l], l, expert_w1, expert_b1,
                       expert_w2, expert_b2, final_norm_g, l == depth - 1)
    return x
```

```python
import functools

import jax
import jax.numpy as jnp
from jax import lax
from jax.experimental import pallas as pl
from jax.experimental.pallas import tpu as pltpu

F32, BF16, I32, U32 = jnp.float32, jnp.bfloat16, jnp.int32, jnp.uint32

HEAD_DIM = 64
CHUNK = 128
RGLRU_C = 8.0
TOP_K = 4
SWIGLU_LIMIT = 7.0
SWIGLU_ALPHA = 1.702
N_MOD = 6
EPS = 1e-6

SUBLANES = 8
LANES = 128
VMEM_LIMIT_BYTES = 56 * 1024 * 1024

MIX_ROWS = 512
ROUTE_ROWS = 256
ROUTE_STEP_WINDOWS = 2
COMBINE_STEP_WINDOWS = 2
PIECE_UNROLL = 4
INLINE_PIECES = 20
EXPERT_ROWS = 512
SEG_ALIGN = SUBLANES
MOD_COLS = 1024


def _rms_normalize(y):
    return y * lax.rsqrt(jnp.mean(y * y, axis=-1, keepdims=True) + EPS)


def _round_up(x, m):
    return (x + m - 1) // m * m


def _mod_kernel(c_ref, w_ref, b_ref, o_ref):
    o_ref[...] = (
        jnp.dot(c_ref[...], w_ref[...], preferred_element_type=F32, precision=lax.Precision.HIGHEST)
        + b_ref[...]
    )


def _adaln_modulation(c, w_ada, b_ada):
    depth, d, n = w_ada.shape
    bsz = c.shape[0]
    rows = -(-bsz // SUBLANES) * SUBLANES
    c_pad = jnp.pad(c, ((0, rows - bsz), (0, 0)))
    out = pl.pallas_call(
        _mod_kernel,
        grid=(depth, n // MOD_COLS),
        in_specs=[
            pl.BlockSpec((rows, d), lambda l, j: (0, 0)),
            pl.BlockSpec((None, d, MOD_COLS), lambda l, j: (l, 0, j)),
            pl.BlockSpec((None, 1, MOD_COLS), lambda l, j: (l, 0, j)),
        ],
        out_specs=pl.BlockSpec((None, rows, MOD_COLS), lambda l, j: (l, 0, j)),
        out_shape=jax.ShapeDtypeStruct((depth, rows, n), F32),
        name="adaln_mod",
    )(c_pad, w_ada, b_ada.reshape(depth, 1, n))
    return out[:, :bsz].reshape(depth, bsz, N_MOD, d)


def _mixer_kernel(x_ref, mod_ref, n1g_ref, win_ref, caw_ref, sguw_ref, sgub_ref, lng_ref, lnb_ref,
                  ccw_ref, ccb_ref, gw_ref, gb_ref, lam_ref, mng_ref, wout_ref,
                  o_ref, cvh_ref, zxh_ref, hc_ref, a_scr, b_scr, *, w_conv, w_sgu, w_lru):
    rows = x_ref.shape[0]
    hist = SUBLANES

    @pl.when(pl.program_id(1) == 0)
    def _():
        cvh_ref[0:hist, :] = jnp.zeros((hist, w_conv), F32)
        zxh_ref[0:hist, :] = jnp.zeros((hist, w_lru), F32)
        hc_ref[...] = jnp.zeros_like(hc_ref)

    x = x_ref[...]
    shift1, scale1, gate1 = mod_ref[0:1, :], mod_ref[1:2, :], mod_ref[2:3, :]
    h = _rms_normalize(x) * n1g_ref[...]
    h = h * (1.0 + scale1) + shift1
    z = jnp.dot(h.astype(BF16), win_ref[...], preferred_element_type=F32)

    o = 0
    z_b = z[:, o:o + w_conv]; o += w_conv
    z_c = z[:, o:o + w_conv]; o += w_conv
    z_v = z[:, o:o + w_conv]; o += w_conv
    z_su = z[:, o:o + w_sgu]; o += w_sgu
    z_sv = z[:, o:o + w_sgu]; o += w_sgu
    z_g = z[:, o:o + w_lru]; o += w_lru
    z_x = z[:, o:o + w_lru]

    cv = z_c * z_v
    cvh_ref[hist:hist + rows, :] = cv
    conv = (caw_ref[0:1, :] * cvh_ref[hist - 2:hist - 2 + rows, :]
            + caw_ref[1:2, :] * cvh_ref[hist - 1:hist - 1 + rows, :]
            + caw_ref[2:3, :] * cv)
    cvh_ref[0:hist, :] = cvh_ref[rows:rows + hist, :]
    y_conv = z_b * conv

    u = jax.nn.gelu(z_su)
    v = jax.nn.gelu(z_sv)
    vc = v - jnp.mean(v, axis=-1, keepdims=True)
    vn = vc * lax.rsqrt(jnp.mean(vc * vc, axis=-1, keepdims=True) + EPS)
    vn = (vn * lng_ref[...] + lnb_ref[...]).astype(BF16)
    n_heads = w_sgu // HEAD_DIM
    wr = lax.broadcasted_iota(I32, (n_heads * CHUNK, CHUNK), 0)
    wc = lax.broadcasted_iota(I32, (n_heads * CHUNK, CHUNK), 1)
    w_causal = jnp.where(wc <= (wr & (CHUNK - 1)), sguw_ref[...], 0.0).astype(BF16)
    lane = lax.broadcasted_iota(I32, (CHUNK, w_sgu), 1)
    mixed = []
    for ci in range(rows // CHUNK):
        res = jnp.dot(w_causal, vn[ci * CHUNK:(ci + 1) * CHUNK, :], preferred_element_type=F32)
        m = res[0:CHUNK, :]
        for hh in range(1, n_heads):
            m = jnp.where(lane >= hh * HEAD_DIM, res[hh * CHUNK:(hh + 1) * CHUNK, :], m)
        mixed.append(m + sgub_ref[...])
    y_sgu = u * jnp.concatenate(mixed, axis=0)

    zxh_ref[hist:hist + rows, :] = z_x
    xr = (ccw_ref[0:1, :] * zxh_ref[hist - 3:hist - 3 + rows, :]
          + ccw_ref[1:2, :] * zxh_ref[hist - 2:hist - 2 + rows, :]
          + ccw_ref[2:3, :] * zxh_ref[hist - 1:hist - 1 + rows, :]
          + ccw_ref[3:4, :] * z_x + ccb_ref[...])
    zxh_ref[0:hist, :] = zxh_ref[rows:rows + hist, :]
    xr_b = xr.astype(BF16)
    half = w_lru // 2
    g0 = jnp.dot(xr_b[:, :half], gw_ref[0], preferred_element_type=F32)
    g1 = jnp.dot(xr_b[:, half:], gw_ref[1], preferred_element_type=F32)
    r_gate = jax.nn.sigmoid(jnp.concatenate([g0[:, :half], g1[:, :half]], axis=-1) + gb_ref[0:1, :])
    i_gate = jax.nn.sigmoid(jnp.concatenate([g0[:, half:], g1[:, half:]], axis=-1) + gb_ref[1:2, :])
    neg_lam = -lam_ref[...]
    softplus = jnp.maximum(neg_lam, 0.0) + jnp.log1p(jnp.exp(-jnp.abs(neg_lam)))
    log_a = (-RGLRU_C) * r_gate * softplus
    a = jnp.exp(log_a)
    a_scr[...] = a
    one_m_a2 = 1.0 - a * a
    b_scr[...] = xr * i_gate * jnp.where(one_m_a2 > 0.0, one_m_a2 * lax.rsqrt(one_m_a2), 0.0)

    row8 = lax.broadcasted_iota(I32, (SUBLANES, w_lru), 0)

    def scan_group(g, carry):
        r0 = pl.multiple_of(g * SUBLANES, SUBLANES)
        a8 = a_scr[pl.ds(r0, SUBLANES), :]
        b8 = b_scr[pl.ds(r0, SUBLANES), :]
        for d in (1, 2, 4):
            keep = row8 >= d
            b8 = jnp.where(keep, a8 * pltpu.roll(b8, d, 0) + b8, b8)
            a8 = jnp.where(keep, a8 * pltpu.roll(a8, d, 0), a8)
        h8 = a8 * carry + b8
        b_scr[pl.ds(r0, SUBLANES), :] = h8
        return jnp.broadcast_to(h8[SUBLANES - 1:SUBLANES, :], (SUBLANES, w_lru))

    hc_ref[...] = lax.fori_loop(0, rows // SUBLANES, scan_group, hc_ref[...], unroll=True)
    y_lru = jax.nn.gelu(z_g) * b_scr[...]

    y = jnp.concatenate([_rms_normalize(y_conv), _rms_normalize(y_sgu), _rms_normalize(y_lru)], axis=-1)
    y = (y * mng_ref[...]).astype(BF16)
    o_ref[...] = x + gate1 * jnp.dot(y, wout_ref[...], preferred_element_type=F32)


def _mixer(x, mod_l, norm1_g, w_in, conv_a_w, sgu_w, sgu_b, sgu_ln_g, sgu_ln_b, conv_c_w, conv_c_b,
           rglru_gate_w, rglru_gate_b, rglru_lambda, mix_norm_g, w_out):
    bsz, seq, d = x.shape
    w_conv = conv_a_w.shape[1]
    w_sgu = sgu_ln_g.shape[0]
    w_lru = rglru_lambda.shape[0]
    d_in = w_in.shape[1]
    n_sgu_heads = sgu_w.shape[0]
    n_lru_heads = rglru_gate_w.shape[1]
    rows = MIX_ROWS
    half = w_lru // 2

    sgu_bias = jnp.repeat(sgu_b.T, HEAD_DIM, axis=1)
    eye = jnp.eye(n_lru_heads, dtype=F32)
    bd = jnp.einsum("ghij,hk->ghikj", rglru_gate_w, eye).reshape(2, w_lru, w_lru)
    gate_w = jnp.stack([
        jnp.concatenate([bd[0, j * half:(j + 1) * half, j * half:(j + 1) * half],
                         bd[1, j * half:(j + 1) * half, j * half:(j + 1) * half]], axis=1)
        for j in range(2)]).astype(BF16)

    full = lambda shape: pl.BlockSpec(shape, lambda b, s: (0,) * len(shape))
    kern = functools.partial(_mixer_kernel, w_conv=w_conv, w_sgu=w_sgu, w_lru=w_lru)
    return pl.pallas_call(
        kern,
        grid=(bsz, seq // rows),
        in_specs=[
            pl.BlockSpec((None, rows, d), lambda b, s: (b, s, 0)),
            pl.BlockSpec((None, N_MOD, d), lambda b, s: (b, 0, 0)),
            full((1, d)),
            full((d, d_in)),
            full(conv_a_w.shape),
            full((n_sgu_heads * CHUNK, CHUNK)),
            full((CHUNK, w_sgu)),
            full((1, w_sgu)),
            full((1, w_sgu)),
            full(conv_c_w.shape),
            full((1, w_lru)),
            full((2, half, 2 * half)),
            full((2, w_lru)),
            full((1, w_lru)),
            full((1, d)),
            full((d, d)),
        ],
        out_specs=pl.BlockSpec((None, rows, d), lambda b, s: (b, s, 0)),
        out_shape=jax.ShapeDtypeStruct((bsz, seq, d), F32),
        scratch_shapes=[
            pltpu.VMEM((rows + SUBLANES, w_conv), F32),
            pltpu.VMEM((rows + SUBLANES, w_lru), F32),
            pltpu.VMEM((SUBLANES, w_lru), F32),
            pltpu.VMEM((rows, w_lru), F32),
            pltpu.VMEM((rows, w_lru), F32),
        ],
        compiler_params=pltpu.CompilerParams(
            dimension_semantics=("arbitrary", "arbitrary"), vmem_limit_bytes=VMEM_LIMIT_BYTES),
        name="mixer",
    )(x, mod_l, norm1_g.reshape(1, d), w_in.astype(BF16), conv_a_w,
      sgu_w.reshape(n_sgu_heads * CHUNK, CHUNK), sgu_bias, sgu_ln_g.reshape(1, w_sgu),
      sgu_ln_b.reshape(1, w_sgu), conv_c_w, conv_c_b.reshape(1, w_lru), gate_w,
      rglru_gate_b.reshape(2, w_lru), rglru_lambda.reshape(1, w_lru), mix_norm_g.reshape(1, d),
      w_out.astype(BF16))


def _pack_bf16_pairs(a, b):
    ua = lax.bitcast_convert_type(a, U32)
    ub = lax.bitcast_convert_type(b, U32)
    return ua | (ub >> 16)


def _unpack_bf16_pairs(p):
    a = lax.bitcast_convert_type(p & jnp.uint32(0xFFFF0000), F32)
    b = lax.bitcast_convert_type(p << 16, F32)
    return a, b


def _stage_rows(n_exp):
    return _round_up(TOP_K * ROUTE_ROWS + n_exp * (SEG_ALIGN - 1), ROUTE_ROWS)


def _modulated_norm(x, g, scale, shift):
    return (_rms_normalize(x) * g) * (1.0 + scale) + shift


def _split_bf16(v):
    hi = v.astype(BF16)
    return hi, (v - hi.astype(F32)).astype(BF16)


def _router_kernel(x_ref, mod_ref, n2g_ref, rwt_ref, rb_ref, xs_ref, pos_ref, cnt_ref):
    for wi in range(xs_ref.shape[0]):
        sl = slice(wi * ROUTE_ROWS, (wi + 1) * ROUTE_ROWS)
        _route_window(x_ref[sl, :], mod_ref, n2g_ref, rwt_ref, rb_ref,
                      xs_ref.at[wi], pos_ref.at[:, sl], cnt_ref.at[wi])


def _route_window(x, mod_ref, n2g_ref, rwt_ref, rb_ref, xs_ref, pos_ref, cnt_ref):
    rows, d = x.shape
    n_exp = rwt_ref.shape[0]
    stage = xs_ref.shape[0]
    half = d // 2
    nt = (((1,), (1,)), ((), ()))

    h2 = _modulated_norm(x, n2g_ref[...], mod_ref[4:5, :], mod_ref[3:4, :])
    h_hi, h_lo = _split_bf16(h2)
    rw_hi, rw_lo = _split_bf16(rwt_ref[...])
    logits = (lax.dot_general(rw_hi, h_hi, nt, preferred_element_type=F32)
              + lax.dot_general(rw_hi, h_lo, nt, preferred_element_type=F32)
              + lax.dot_general(rw_lo, h_hi, nt, preferred_element_type=F32))
    logits = logits + rb_ref[...]

    e_iota = lax.broadcasted_iota(I32, (n_exp, rows), 0).astype(F32)
    work = logits
    tops, hots, idxs = [], [], []
    for k in range(TOP_K):
        m = jnp.max(work, axis=0, keepdims=True)
        idx = jnp.min(jnp.where(work == m, e_iota, float(n_exp)), axis=0, keepdims=True)
        hot = e_iota == idx
        work = jnp.where(hot, -jnp.inf, work)
        tops.append(m)
        hots.append(hot)
        idxs.append(idx)

    p = [jnp.exp(t - tops[0]) for t in tops]
    denom = p[0] + p[1] + p[2] + p[3]

    sel = jnp.zeros((n_exp, rows), F32)
    for hot in hots:
        sel = sel + hot.astype(F32)
    s_io = lax.broadcasted_iota(I32, (rows, rows), 0)
    t_io = lax.broadcasted_iota(I32, (rows, rows), 1)
    before = (s_io < t_io).astype(BF16)
    rank_e = jnp.dot(sel.astype(BF16), before, preferred_element_type=F32)
    n_e = jnp.sum(sel, axis=1, keepdims=True)
    seg_len = jnp.floor((n_e + (SEG_ALIGN - 1)) * (1.0 / SEG_ALIGN)) * SEG_ALIGN
    incl = jnp.broadcast_to(seg_len, (n_exp, LANES))
    e_row = lax.broadcasted_iota(I32, (n_exp, LANES), 0)
    step = 1
    while step < n_exp:
        incl = incl + jnp.where(e_row >= step, pltpu.roll(incl, step, 0), 0.0)
        step *= 2
    dest_e = (incl[:, 0:1] - seg_len) + rank_e

    r_iota = lax.broadcasted_iota(I32, (stage, rows), 0).astype(F32)
    onehot = None
    side = []
    for k in range(TOP_K):
        pos_k = jnp.sum(jnp.where(hots[k], dest_e, 0.0), axis=0, keepdims=True)
        pos_ref[k:k + 1, :] = pos_k.astype(I32)
        eq = r_iota == pos_k
        onehot = eq if onehot is None else (onehot | eq)
        side.append(p[k] / denom)
    perm = jnp.where(onehot, 1.0, 0.0).astype(BF16)

    w_rows = jnp.concatenate(side, axis=0)
    w_hi = w_rows.astype(BF16).astype(F32)
    side_rows = jnp.concatenate(
        [w_hi, w_rows - w_hi, jnp.concatenate(idxs, axis=0),
         jnp.zeros((LANES - 3 * TOP_K, rows), F32)], axis=0)
    side_cols = jnp.transpose(side_rows).astype(BF16)
    staged = jnp.dot(perm, jnp.concatenate([h_hi, side_cols], axis=1),
                     preferred_element_type=F32)
    xs_ref[:, 0:half] = _pack_bf16_pairs(staged[:, :half], staged[:, half:d])
    xs_ref[:, half:] = lax.bitcast_convert_type(staged[:, d:], U32)
    cnt_ref[...] = jnp.broadcast_to(n_e, (n_exp, LANES)).astype(I32)


def _router(x1, mod_l, norm2_g, router_w, router_b):
    bsz, seq, d = x1.shape
    n_tok = bsz * seq
    n_exp = router_w.shape[1]
    n_win = n_tok // ROUTE_ROWS
    wins = ROUTE_STEP_WINDOWS
    rows = wins * ROUTE_ROWS
    tiles_per_seq = seq // rows
    stage = _stage_rows(n_exp)
    return pl.pallas_call(
        _router_kernel,
        grid=(n_win // wins,),
        in_specs=[
            pl.BlockSpec((rows, d), lambda i: (i, 0)),
            pl.BlockSpec((None, N_MOD, d), lambda i: (i // tiles_per_seq, 0, 0)),
            pl.BlockSpec((1, d), lambda i: (0, 0)),
            pl.BlockSpec((n_exp, d), lambda i: (0, 0)),
            pl.BlockSpec((n_exp, 1), lambda i: (0, 0)),
        ],
        out_specs=[
            pl.BlockSpec((wins, stage, d // 2 + LANES), lambda i: (i, 0, 0)),
            pl.BlockSpec((TOP_K, rows), lambda i: (0, i)),
            pl.BlockSpec((wins, n_exp, LANES), lambda i: (i, 0, 0)),
        ],
        out_shape=[
            jax.ShapeDtypeStruct((n_win, stage, d // 2 + LANES), U32),
            jax.ShapeDtypeStruct((TOP_K, n_tok), I32),
            jax.ShapeDtypeStruct((n_win, n_exp, LANES), I32),
        ],
        compiler_params=pltpu.CompilerParams(
            dimension_semantics=("arbitrary",), vmem_limit_bytes=VMEM_LIMIT_BYTES),
        name="router",
    )(x1.reshape(n_tok, d), mod_l, norm2_g.reshape(1, d), router_w.T, router_b.reshape(n_exp, 1))


def _expert_kernel(be_sm, first_sm, nact_sm, valid_sm, plo_sm, phi_sm, psrc_sm, pdst_sm, pn_sm, csrc_sm,
                   cn_sm, used_sm, xs_ref, w1_ref, b1_ref, w2_ref, b2_ref, ys_ref,
                   w1b_ref, w2b_ref, xbuf, ybuf, zbuf, in_sem, out_sem, tail_sem, *, n_win, stage):
    g = pl.program_id(0)
    n_steps = pl.num_programs(0)
    nact = nact_sm[0]
    d_ff = w2_ref.shape[0]
    half = ybuf.shape[2]

    def pieces(blk, fn, enabled=True, inline=0):
        lo, hi = plo_sm[blk], phi_sm[blk]

        def group(first):
            idx = [first + j for j in range(PIECE_UNROLL)]
            rows = [(p, pn_sm[p], psrc_sm[p], pdst_sm[p]) for p in idx]
            for p, n, src, dst in rows:
                @pl.when(enabled & (p < hi) & (n > 0))
                def _():
                    fn(pl.multiple_of(src, SEG_ALIGN), pl.multiple_of(dst, SEG_ALIGN),
                       pl.multiple_of(n, SEG_ALIGN))

        carried, carried_src = cn_sm[blk], csrc_sm[blk]

        @pl.when(enabled & (carried > 0))
        def _():
            fn(pl.multiple_of(carried_src, SEG_ALIGN), 0, pl.multiple_of(carried, SEG_ALIGN))

        for j in range(0, inline, PIECE_UNROLL):
            group(lo + j)

        def body(i, c):
            group(lo + inline + i * PIECE_UNROLL)
            return c
        left = jnp.maximum(hi - lo - inline, 0)
        trips = lax.shift_right_logical(left + (PIECE_UNROLL - 1), PIECE_UNROLL.bit_length() - 1)
        lax.fori_loop(0, jnp.where(enabled, trips, 0), body, 0)

    def in_copy(buf, src, dst, n):
        return pltpu.make_async_copy(
            xs_ref.at[pl.ds(src, n), :], xbuf.at[buf, pl.ds(dst, n), :], in_sem.at[buf])

    def out_copy(buf, src, dst, n):
        return pltpu.make_async_copy(
            ybuf.at[buf, pl.ds(dst, n), :], ys_ref.at[pl.ds(src, n), :], out_sem.at[buf])

    def valid_rows(blk):
        return pl.multiple_of(valid_sm[blk], SEG_ALIGN)

    def tail_copy(w):
        used = pl.multiple_of(used_sm[w], SEG_ALIGN)
        n = pl.multiple_of(stage - used, SEG_ALIGN)
        row = pl.multiple_of(w * stage + used, SEG_ALIGN)
        return pltpu.make_async_copy(zbuf.at[pl.ds(0, n), :], ys_ref.at[pl.ds(row, n), :], tail_sem)

    def for_windows(fn):
        def body(w, c):
            fn(w)
            return c
        lax.fori_loop(0, n_win, body, 0)

    @pl.when(g == 0)
    def _():
        xbuf[...] = jnp.zeros_like(xbuf)
        zbuf[...] = jnp.zeros_like(zbuf)
        for_windows(lambda w: tail_copy(w).start())
        pieces(0, lambda *a: in_copy(0, *a).start())

    @pl.when((g >= 2) & (g - 2 < nact))
    def _():
        out_copy(g % 2, 0, 0, valid_rows(g - 2)).wait()

    @pl.when(first_sm[g] == 1)
    def _():
        w1b_ref[...] = w1_ref[...].astype(BF16)
        w2b_ref[...] = w2_ref[...].astype(BF16)

    @pl.when(g < nact)
    def _():
        buf = g % 2
        pieces(g + 1, lambda *a: in_copy(1 - buf, *a).start(), enabled=g + 1 < nact, inline=INLINE_PIECES)
        in_copy(buf, 0, 0, valid_rows(g)).wait()
        packed = xbuf[buf]
        keep = lax.broadcasted_iota(I32, (EXPERT_ROWS, 1), 0) < valid_sm[g]
        xa, xb = _unpack_bf16_pairs(packed[:, :half])
        x = jnp.where(keep, jnp.concatenate([xa, xb], axis=-1), 0.0).astype(BF16)
        side = lax.bitcast_convert_type(packed[:, half:], F32)
        expert = be_sm[g].astype(F32)
        wcol = jnp.zeros((EXPERT_ROWS, 1), F32)
        for k in range(TOP_K):
            w_k = side[:, k:k + 1] + side[:, TOP_K + k:TOP_K + k + 1]
            wcol = jnp.where(side[:, 2 * TOP_K + k:2 * TOP_K + k + 1] == expert, w_k, wcol)
        wcol = jnp.where(keep, wcol, 0.0)
        hgu = jnp.dot(x, w1b_ref[...], preferred_element_type=F32) + b1_ref[...]
        gt = jnp.minimum(hgu[:, :d_ff], SWIGLU_LIMIT)
        up = jnp.clip(hgu[:, d_ff:], -SWIGLU_LIMIT, SWIGLU_LIMIT)
        act = (up + 1.0) * (gt * jax.nn.sigmoid(SWIGLU_ALPHA * gt))
        y = (jnp.dot(act.astype(BF16), w2b_ref[...], preferred_element_type=F32) + b2_ref[...]) * wcol
        y = y.astype(BF16).astype(F32)
        ybuf[buf] = _pack_bf16_pairs(y[:, :half], y[:, half:])
        pieces(g, lambda *a: out_copy(buf, *a).start(), inline=INLINE_PIECES)

    @pl.when(g == n_steps - 1)
    def _():
        for back in (1, 0):
            blk = g - back

            @pl.when((blk >= 0) & (blk < nact))
            def _():
                out_copy(blk % 2, 0, 0, valid_rows(blk)).wait()
        for_windows(lambda w: tail_copy(w).wait())


def _experts(xs, tables, layer, w1, b1, w2, b2):
    n_win, stage, xw = xs.shape
    depth, n_exp, d, d_hid = w1.shape
    d_ff = w2.shape[2]
    half = d // 2
    n_blocks = tables[0].shape[0]
    wmap = lambda g, be, *_: (layer, be[g], 0, 0)
    grid_spec = pltpu.PrefetchScalarGridSpec(
        num_scalar_prefetch=len(tables),
        grid=(n_blocks,),
        in_specs=[
            pl.BlockSpec(memory_space=pl.ANY),
            pl.BlockSpec((None, None, d, d_hid), wmap),
            pl.BlockSpec((None, None, 1, d_hid), wmap),
            pl.BlockSpec((None, None, d_ff, d), wmap),
            pl.BlockSpec((None, None, 1, d), wmap),
        ],
        out_specs=pl.BlockSpec(memory_space=pl.ANY),
        scratch_shapes=[
            pltpu.VMEM((d, d_hid), BF16),
            pltpu.VMEM((d_ff, d), BF16),
            pltpu.VMEM((2, EXPERT_ROWS, xw), U32),
            pltpu.VMEM((2, EXPERT_ROWS, half), U32),
            pltpu.VMEM((stage - TOP_K * ROUTE_ROWS, half), U32),
            pltpu.SemaphoreType.DMA((2,)),
            pltpu.SemaphoreType.DMA((2,)),
            pltpu.SemaphoreType.DMA,
        ],
    )
    ys = pl.pallas_call(
        functools.partial(_expert_kernel, n_win=n_win, stage=stage),
        grid_spec=grid_spec,
        out_shape=jax.ShapeDtypeStruct((n_win * stage, half), U32),
        compiler_params=pltpu.CompilerParams(
            dimension_semantics=("arbitrary",), vmem_limit_bytes=VMEM_LIMIT_BYTES),
        name="experts",
    )(*tables, xs.reshape(n_win * stage, xw), w1, b1.reshape(depth, n_exp, 1, d_hid), w2,
      b2.reshape(depth, n_exp, 1, d))
    return ys.reshape(n_win, stage, half)


def _combine_kernel(x_ref, mod_ref, pos_ref, fng_ref, ys_ref, o_ref, *, final_norm):
    wins, stage = ys_ref.shape[0], ys_ref.shape[1]
    rows = ROUTE_ROWS
    for wi in range(wins):
        sl = slice(wi * rows, (wi + 1) * rows)
        pos_rows = jnp.concatenate(
            [pos_ref[:, sl].astype(F32), jnp.full((LANES - TOP_K, rows), -1.0, F32)], axis=0)
        pos_t = jnp.transpose(pos_rows)
        c_iota = lax.broadcasted_iota(I32, (rows, stage), 1).astype(F32)
        hit = c_iota == pos_t[:, 0:1]
        for k in range(1, TOP_K):
            hit = hit | (c_iota == pos_t[:, k:k + 1])
        unperm = jnp.where(hit, 1.0, 0.0).astype(BF16)
        ya, yb = _unpack_bf16_pairs(ys_ref[wi])
        moe = jnp.concatenate(
            [jnp.dot(unperm, ya.astype(BF16), preferred_element_type=F32),
             jnp.dot(unperm, yb.astype(BF16), preferred_element_type=F32)], axis=-1)
        out = x_ref[sl, :] + mod_ref[5:6, :] * moe
        if final_norm:
            out = _rms_normalize(out) * fng_ref[...]
        o_ref[sl, :] = out


def _combine(x1, mod_l, pos, ys, final_norm_g, final_norm):
    bsz, seq, d = x1.shape
    n_tok = bsz * seq
    wins = COMBINE_STEP_WINDOWS
    rows = wins * ROUTE_ROWS
    tiles_per_seq = seq // rows
    n_win, stage, half = ys.shape
    out = pl.pallas_call(
        functools.partial(_combine_kernel, final_norm=final_norm),
        grid=(n_win // wins,),
        in_specs=[
            pl.BlockSpec((rows, d), lambda i: (i, 0)),
            pl.BlockSpec((None, N_MOD, d), lambda i: (i // tiles_per_seq, 0, 0)),
            pl.BlockSpec((TOP_K, rows), lambda i: (0, i)),
            pl.BlockSpec((1, d), lambda i: (0, 0)),
            pl.BlockSpec((wins, stage, half), lambda i: (i, 0, 0)),
        ],
        out_specs=pl.BlockSpec((rows, d), lambda i: (i, 0)),
        out_shape=jax.ShapeDtypeStruct((n_tok, d), F32),
        compiler_params=pltpu.CompilerParams(
            dimension_semantics=("arbitrary",), vmem_limit_bytes=VMEM_LIMIT_BYTES),
        name="combine",
    )(x1.reshape(n_tok, d), mod_l, pos, final_norm_g.reshape(1, d), ys)
    return out.reshape(bsz, seq, d)


def _segment_tables(cnt, stage):
    n_win, n_exp = cnt.shape
    seg_len = _round_up(cnt, SEG_ALIGN)
    seg_src = jnp.cumsum(seg_len, axis=1) - seg_len
    used = jnp.sum(seg_len, axis=1)
    seg_off = jnp.cumsum(seg_len, axis=0) - seg_len
    reg_len = jnp.sum(seg_len, axis=0)
    reg_pad = _round_up(reg_len, EXPERT_ROWS)
    pad_end = jnp.cumsum(reg_pad)
    base = pad_end - reg_pad
    max_rows = n_win * (TOP_K * ROUTE_ROWS + n_exp * (SEG_ALIGN - 1)) + n_exp * (EXPERT_ROWS - SEG_ALIGN)
    n_blocks = -(-max_rows // EXPERT_ROWS)
    n_active = pad_end[-1] // EXPERT_ROWS
    gidx = jnp.arange(n_blocks, dtype=I32)
    blk_row = (gidx * EXPERT_ROWS)[:, None]
    in_region = (base[None, :] <= blk_row) & (blk_row < pad_end[None, :])
    valid = jnp.sum(jnp.where(in_region, jnp.clip((base + reg_len)[None, :] - blk_row, 0, EXPERT_ROWS), 0), axis=1)
    first = jnp.any(in_region & (base[None, :] == blk_row), axis=1)
    last_row = (jnp.minimum(gidx, n_active - 1) * EXPERT_ROWS)[:, None]
    block_e = jnp.minimum(jnp.sum(pad_end[None, :] <= last_row, axis=1), n_exp - 1)

    glob = (base[None, :] + seg_off).T.reshape(-1)
    seg_n = seg_len.T.reshape(-1)
    src = (jnp.arange(n_win, dtype=I32)[:, None] * stage + seg_src).T.reshape(-1)
    head_blk = glob // EXPERT_ROWS
    head_dst = glob % EXPERT_ROWS
    head_n = jnp.minimum(seg_n, EXPERT_ROWS - head_dst)
    carry_n = seg_n - head_n
    p_lo = jnp.sum(head_blk[None, :] < gidx[:, None], axis=1)
    p_hi = jnp.sum(head_blk[None, :] <= gidx[:, None], axis=1)
    carried = (head_blk[None, :] + 1 == gidx[:, None]) & (carry_n[None, :] > 0)
    c_n = jnp.sum(jnp.where(carried, carry_n[None, :], 0), axis=1)
    c_src = jnp.sum(jnp.where(carried, (src + head_n)[None, :], 0), axis=1)
    p_src, p_dst, p_n = (jnp.pad(t, (0, INLINE_PIECES + PIECE_UNROLL)) for t in (src, head_dst, head_n))
    tables = (block_e, first, n_active.reshape(1), valid, p_lo, p_hi, p_src, p_dst, p_n, c_src, c_n, used)
    return tuple(t.astype(I32) for t in tables)


def _moe_layer(x1, mod_l, norm2_g, router_w, router_b, layer, w1, b1, w2, b2, final_norm_g, final_norm):
    xs, pos, cnt = _router(x1, mod_l, norm2_g, router_w, router_b)
    tables = _segment_tables(cnt[:, :, 0], xs.shape[1])
    ys = _experts(xs, tables, layer, w1, b1, w2, b2)
    return _combine(x1, mod_l, pos, ys, final_norm_g, final_norm)


def kernel(x, c, norm1_g, norm2_g, w_ada, b_ada, w_in, conv_a_w, sgu_w, sgu_b, sgu_ln_g, sgu_ln_b,
           conv_c_w, conv_c_b, rglru_gate_w, rglru_gate_b, rglru_lambda, mix_norm_g, w_out, router_w,
           router_b, expert_w1, expert_b1, expert_w2, expert_b2, final_norm_g):
    depth = w_ada.shape[0]
    mod = _adaln_modulation(c, w_ada, b_ada)
    for l in range(depth):
        x1 = _mixer(x, mod[l], norm1_g[l], w_in[l], conv_a_w[l], sgu_w[l], sgu_b[l], sgu_ln_g[l],
                    sgu_ln_b[l], conv_c_w[l], conv_c_b[l], rglru_gate_w[l], rglru_gate_b[l],
                    rglru_lambda[l], mix_norm_g[l], w_out[l])
        x = _moe_layer(x1, mod[l], norm2_g[l], router_w[l], router_b[l], l, expert_w1, expert_b1,
                       expert_w2, expert_b2, final_norm_g, l == depth - 1)
    return x
```

```python
import functools

import jax
import jax.numpy as jnp
from jax import lax
from jax.experimental import pallas as pl
from jax.experimental.pallas import tpu as pltpu

F32, BF16, I32, U32 = jnp.float32, jnp.bfloat16, jnp.int32, jnp.uint32

HEAD_DIM = 64
CHUNK = 128
RGLRU_C = 8.0
TOP_K = 4
SWIGLU_LIMIT = 7.0
SWIGLU_ALPHA = 1.702
N_MOD = 6
EPS = 1e-6

SUBLANES = 8
LANES = 128
VMEM_LIMIT_BYTES = 56 * 1024 * 1024

MIX_ROWS = 512
ROUTE_ROWS = 256
ROUTE_STEP_WINDOWS = 4
COMBINE_STEP_WINDOWS = 4
PIECE_UNROLL = 4
INLINE_PIECES = 20
EXPERT_ROWS = 512
SEG_ALIGN = SUBLANES
MOD_COLS = 1024


def _rms_normalize(y):
    return y * lax.rsqrt(jnp.mean(y * y, axis=-1, keepdims=True) + EPS)


def _round_up(x, m):
    return (x + m - 1) // m * m


def _mod_kernel(c_ref, w_ref, b_ref, o_ref):
    o_ref[...] = (
        jnp.dot(c_ref[...], w_ref[...], preferred_element_type=F32, precision=lax.Precision.HIGHEST)
        + b_ref[...]
    )


def _adaln_modulation(c, w_ada, b_ada):
    depth, d, n = w_ada.shape
    bsz = c.shape[0]
    rows = -(-bsz // SUBLANES) * SUBLANES
    c_pad = jnp.pad(c, ((0, rows - bsz), (0, 0)))
    out = pl.pallas_call(
        _mod_kernel,
        grid=(depth, n // MOD_COLS),
        in_specs=[
            pl.BlockSpec((rows, d), lambda l, j: (0, 0)),
            pl.BlockSpec((None, d, MOD_COLS), lambda l, j: (l, 0, j)),
            pl.BlockSpec((None, 1, MOD_COLS), lambda l, j: (l, 0, j)),
        ],
        out_specs=pl.BlockSpec((None, rows, MOD_COLS), lambda l, j: (l, 0, j)),
        out_shape=jax.ShapeDtypeStruct((depth, rows, n), F32),
        name="adaln_mod",
    )(c_pad, w_ada, b_ada.reshape(depth, 1, n))
    return out[:, :bsz].reshape(depth, bsz, N_MOD, d)


def _mixer_kernel(x_ref, mod_ref, n1g_ref, win_ref, caw_ref, sguw_ref, sgub_ref, lng_ref, lnb_ref,
                  ccw_ref, ccb_ref, gw_ref, gb_ref, lam_ref, mng_ref, wout_ref,
                  o_ref, cvh_ref, zxh_ref, hc_ref, a_scr, b_scr, *, w_conv, w_sgu, w_lru):
    rows = x_ref.shape[0]
    hist = SUBLANES

    @pl.when(pl.program_id(1) == 0)
    def _():
        cvh_ref[0:hist, :] = jnp.zeros((hist, w_conv), F32)
        zxh_ref[0:hist, :] = jnp.zeros((hist, w_lru), F32)
        hc_ref[...] = jnp.zeros_like(hc_ref)

    x = x_ref[...]
    shift1, scale1, gate1 = mod_ref[0:1, :], mod_ref[1:2, :], mod_ref[2:3, :]
    h = _rms_normalize(x) * n1g_ref[...]
    h = h * (1.0 + scale1) + shift1
    z = jnp.dot(h.astype(BF16), win_ref[...], preferred_element_type=F32)

    o = 0
    z_b = z[:, o:o + w_conv]; o += w_conv
    z_c = z[:, o:o + w_conv]; o += w_conv
    z_v = z[:, o:o + w_conv]; o += w_conv
    z_su = z[:, o:o + w_sgu]; o += w_sgu
    z_sv = z[:, o:o + w_sgu]; o += w_sgu
    z_g = z[:, o:o + w_lru]; o += w_lru
    z_x = z[:, o:o + w_lru]

    cv = z_c * z_v
    cvh_ref[hist:hist + rows, :] = cv
    conv = (caw_ref[0:1, :] * cvh_ref[hist - 2:hist - 2 + rows, :]
            + caw_ref[1:2, :] * cvh_ref[hist - 1:hist - 1 + rows, :]
            + caw_ref[2:3, :] * cv)
    cvh_ref[0:hist, :] = cvh_ref[rows:rows + hist, :]
    y_conv = z_b * conv

    u = jax.nn.gelu(z_su)
    v = jax.nn.gelu(z_sv)
    vc = v - jnp.mean(v, axis=-1, keepdims=True)
    vn = vc * lax.rsqrt(jnp.mean(vc * vc, axis=-1, keepdims=True) + EPS)
    vn = (vn * lng_ref[...] + lnb_ref[...]).astype(BF16)
    n_heads = w_sgu // HEAD_DIM
    wr = lax.broadcasted_iota(I32, (n_heads * CHUNK, CHUNK), 0)
    wc = lax.broadcasted_iota(I32, (n_heads * CHUNK, CHUNK), 1)
    w_causal = jnp.where(wc <= (wr & (CHUNK - 1)), sguw_ref[...], 0.0).astype(BF16)
    lane = lax.broadcasted_iota(I32, (CHUNK, w_sgu), 1)
    mixed = []
    for ci in range(rows // CHUNK):
        res = jnp.dot(w_causal, vn[ci * CHUNK:(ci + 1) * CHUNK, :], preferred_element_type=F32)
        m = res[0:CHUNK, :]
        for hh in range(1, n_heads):
            m = jnp.where(lane >= hh * HEAD_DIM, res[hh * CHUNK:(hh + 1) * CHUNK, :], m)
        mixed.append(m + sgub_ref[...])
    y_sgu = u * jnp.concatenate(mixed, axis=0)

    zxh_ref[hist:hist + rows, :] = z_x
    xr = (ccw_ref[0:1, :] * zxh_ref[hist - 3:hist - 3 + rows, :]
          + ccw_ref[1:2, :] * zxh_ref[hist - 2:hist - 2 + rows, :]
          + ccw_ref[2:3, :] * zxh_ref[hist - 1:hist - 1 + rows, :]
          + ccw_ref[3:4, :] * z_x + ccb_ref[...])
    zxh_ref[0:hist, :] = zxh_ref[rows:rows + hist, :]
    xr_b = xr.astype(BF16)
    half = w_lru // 2
    g0 = jnp.dot(xr_b[:, :half], gw_ref[0], preferred_element_type=F32)
    g1 = jnp.dot(xr_b[:, half:], gw_ref[1], preferred_element_type=F32)
    r_gate = jax.nn.sigmoid(jnp.concatenate([g0[:, :half], g1[:, :half]], axis=-1) + gb_ref[0:1, :])
    i_gate = jax.nn.sigmoid(jnp.concatenate([g0[:, half:], g1[:, half:]], axis=-1) + gb_ref[1:2, :])
    neg_lam = -lam_ref[...]
    softplus = jnp.maximum(neg_lam, 0.0) + jnp.log1p(jnp.exp(-jnp.abs(neg_lam)))
    log_a = (-RGLRU_C) * r_gate * softplus
    a = jnp.exp(log_a)
    a_scr[...] = a
    one_m_a2 = 1.0 - a * a
    b_scr[...] = xr * i_gate * jnp.where(one_m_a2 > 0.0, one_m_a2 * lax.rsqrt(one_m_a2), 0.0)

    row8 = lax.broadcasted_iota(I32, (SUBLANES, w_lru), 0)

    def scan_group(g, carry):
        r0 = pl.multiple_of(g * SUBLANES, SUBLANES)
        a8 = a_scr[pl.ds(r0, SUBLANES), :]
        b8 = b_scr[pl.ds(r0, SUBLANES), :]
        for d in (1, 2, 4):
            keep = row8 >= d
            b8 = jnp.where(keep, a8 * pltpu.roll(b8, d, 0) + b8, b8)
            a8 = jnp.where(keep, a8 * pltpu.roll(a8, d, 0), a8)
        h8 = a8 * carry + b8
        b_scr[pl.ds(r0, SUBLANES), :] = h8
        return jnp.broadcast_to(h8[SUBLANES - 1:SUBLANES, :], (SUBLANES, w_lru))

    hc_ref[...] = lax.fori_loop(0, rows // SUBLANES, scan_group, hc_ref[...], unroll=True)
    y_lru = jax.nn.gelu(z_g) * b_scr[...]

    y = jnp.concatenate([_rms_normalize(y_conv), _rms_normalize(y_sgu), _rms_normalize(y_lru)], axis=-1)
    y = (y * mng_ref[...]).astype(BF16)
    o_ref[...] = x + gate1 * jnp.dot(y, wout_ref[...], preferred_element_type=F32)


def _mixer(x, mod_l, norm1_g, w_in, conv_a_w, sgu_w, sgu_b, sgu_ln_g, sgu_ln_b, conv_c_w, conv_c_b,
           rglru_gate_w, rglru_gate_b, rglru_lambda, mix_norm_g, w_out):
    bsz, seq, d = x.shape
    w_conv = conv_a_w.shape[1]
    w_sgu = sgu_ln_g.shape[0]
    w_lru = rglru_lambda.shape[0]
    d_in = w_in.shape[1]
    n_sgu_heads = sgu_w.shape[0]
    n_lru_heads = rglru_gate_w.shape[1]
    rows = MIX_ROWS
    half = w_lru // 2

    sgu_bias = jnp.repeat(sgu_b.T, HEAD_DIM, axis=1)
    eye = jnp.eye(n_lru_heads, dtype=F32)
    bd = jnp.einsum("ghij,hk->ghikj", rglru_gate_w, eye).reshape(2, w_lru, w_lru)
    gate_w = jnp.stack([
        jnp.concatenate([bd[0, j * half:(j + 1) * half, j * half:(j + 1) * half],
                         bd[1, j * half:(j + 1) * half, j * half:(j + 1) * half]], axis=1)
        for j in range(2)]).astype(BF16)

    full = lambda shape: pl.BlockSpec(shape, lambda b, s: (0,) * len(shape))
    kern = functools.partial(_mixer_kernel, w_conv=w_conv, w_sgu=w_sgu, w_lru=w_lru)
    return pl.pallas_call(
        kern,
        grid=(bsz, seq // rows),
        in_specs=[
            pl.BlockSpec((None, rows, d), lambda b, s: (b, s, 0)),
            pl.BlockSpec((None, N_MOD, d), lambda b, s: (b, 0, 0)),
            full((1, d)),
            full((d, d_in)),
            full(conv_a_w.shape),
            full((n_sgu_heads * CHUNK, CHUNK)),
            full((CHUNK, w_sgu)),
            full((1, w_sgu)),
            full((1, w_sgu)),
            full(conv_c_w.shape),
            full((1, w_lru)),
            full((2, half, 2 * half)),
            full((2, w_lru)),
            full((1, w_lru)),
            full((1, d)),
            full((d, d)),
        ],
        out_specs=pl.BlockSpec((None, rows, d), lambda b, s: (b, s, 0)),
        out_shape=jax.ShapeDtypeStruct((bsz, seq, d), F32),
        scratch_shapes=[
            pltpu.VMEM((rows + SUBLANES, w_conv), F32),
            pltpu.VMEM((rows + SUBLANES, w_lru), F32),
            pltpu.VMEM((SUBLANES, w_lru), F32),
            pltpu.VMEM((rows, w_lru), F32),
            pltpu.VMEM((rows, w_lru), F32),
        ],
        compiler_params=pltpu.CompilerParams(
            dimension_semantics=("arbitrary", "arbitrary"), vmem_limit_bytes=VMEM_LIMIT_BYTES),
        name="mixer",
    )(x, mod_l, norm1_g.reshape(1, d), w_in.astype(BF16), conv_a_w,
      sgu_w.reshape(n_sgu_heads * CHUNK, CHUNK), sgu_bias, sgu_ln_g.reshape(1, w_sgu),
      sgu_ln_b.reshape(1, w_sgu), conv_c_w, conv_c_b.reshape(1, w_lru), gate_w,
      rglru_gate_b.reshape(2, w_lru), rglru_lambda.reshape(1, w_lru), mix_norm_g.reshape(1, d),
      w_out.astype(BF16))


def _pack_bf16_pairs(a, b):
    ua = lax.bitcast_convert_type(a, U32)
    ub = lax.bitcast_convert_type(b, U32)
    return ua | (ub >> 16)


def _unpack_bf16_pairs(p):
    a = lax.bitcast_convert_type(p & jnp.uint32(0xFFFF0000), F32)
    b = lax.bitcast_convert_type(p << 16, F32)
    return a, b


def _stage_rows(n_exp):
    return _round_up(TOP_K * ROUTE_ROWS + n_exp * (SEG_ALIGN - 1), ROUTE_ROWS)


def _modulated_norm(x, g, scale, shift):
    return (_rms_normalize(x) * g) * (1.0 + scale) + shift


def _split_bf16(v):
    hi = v.astype(BF16)
    return hi, (v - hi.astype(F32)).astype(BF16)


def _router_kernel(x_ref, mod_ref, n2g_ref, rwt_ref, rb_ref, xs_ref, pos_ref, cnt_ref):
    for wi in range(xs_ref.shape[0]):
        sl = slice(wi * ROUTE_ROWS, (wi + 1) * ROUTE_ROWS)
        _route_window(x_ref[sl, :], mod_ref, n2g_ref, rwt_ref, rb_ref,
                      xs_ref.at[wi], pos_ref.at[:, sl], cnt_ref.at[wi])


def _route_window(x, mod_ref, n2g_ref, rwt_ref, rb_ref, xs_ref, pos_ref, cnt_ref):
    rows, d = x.shape
    n_exp = rwt_ref.shape[0]
    stage = xs_ref.shape[0]
    half = d // 2
    nt = (((1,), (1,)), ((), ()))

    h2 = _modulated_norm(x, n2g_ref[...], mod_ref[4:5, :], mod_ref[3:4, :])
    h_hi, h_lo = _split_bf16(h2)
    rw_hi, rw_lo = _split_bf16(rwt_ref[...])
    logits = (lax.dot_general(rw_hi, h_hi, nt, preferred_element_type=F32)
              + lax.dot_general(rw_hi, h_lo, nt, preferred_element_type=F32)
              + lax.dot_general(rw_lo, h_hi, nt, preferred_element_type=F32))
    logits = logits + rb_ref[...]

    e_iota = lax.broadcasted_iota(I32, (n_exp, rows), 0).astype(F32)
    work = logits
    tops, hots, idxs = [], [], []
    for k in range(TOP_K):
        m = jnp.max(work, axis=0, keepdims=True)
        idx = jnp.min(jnp.where(work == m, e_iota, float(n_exp)), axis=0, keepdims=True)
        hot = e_iota == idx
        work = jnp.where(hot, -jnp.inf, work)
        tops.append(m)
        hots.append(hot)
        idxs.append(idx)

    p = [jnp.exp(t - tops[0]) for t in tops]
    denom = p[0] + p[1] + p[2] + p[3]

    sel = jnp.zeros((n_exp, rows), F32)
    for hot in hots:
        sel = sel + hot.astype(F32)
    s_io = lax.broadcasted_iota(I32, (rows, rows), 0)
    t_io = lax.broadcasted_iota(I32, (rows, rows), 1)
    before = (s_io < t_io).astype(BF16)
    rank_e = jnp.dot(sel.astype(BF16), before, preferred_element_type=F32)
    n_e = jnp.sum(sel, axis=1, keepdims=True)
    seg_len = jnp.floor((n_e + (SEG_ALIGN - 1)) * (1.0 / SEG_ALIGN)) * SEG_ALIGN
    incl = jnp.broadcast_to(seg_len, (n_exp, LANES))
    e_row = lax.broadcasted_iota(I32, (n_exp, LANES), 0)
    step = 1
    while step < n_exp:
        incl = incl + jnp.where(e_row >= step, pltpu.roll(incl, step, 0), 0.0)
        step *= 2
    dest_e = (incl[:, 0:1] - seg_len) + rank_e

    r_iota = lax.broadcasted_iota(I32, (stage, rows), 0).astype(F32)
    onehot = None
    side = []
    for k in range(TOP_K):
        pos_k = jnp.sum(jnp.where(hots[k], dest_e, 0.0), axis=0, keepdims=True)
        pos_ref[k:k + 1, :] = pos_k.astype(I32)
        eq = r_iota == pos_k
        onehot = eq if onehot is None else (onehot | eq)
        side.append(p[k] / denom)
    perm = jnp.where(onehot, 1.0, 0.0).astype(BF16)

    w_rows = jnp.concatenate(side, axis=0)
    w_hi = w_rows.astype(BF16).astype(F32)
    side_rows = jnp.concatenate(
        [w_hi, w_rows - w_hi, jnp.concatenate(idxs, axis=0),
         jnp.zeros((LANES - 3 * TOP_K, rows), F32)], axis=0)
    side_cols = jnp.transpose(side_rows).astype(BF16)
    staged = jnp.dot(perm, jnp.concatenate([h_hi, side_cols], axis=1),
                     preferred_element_type=F32)
    xs_ref[:, 0:half] = _pack_bf16_pairs(staged[:, :half], staged[:, half:d])
    xs_ref[:, half:] = lax.bitcast_convert_type(staged[:, d:], U32)
    cnt_ref[...] = jnp.broadcast_to(n_e, (n_exp, LANES)).astype(I32)


def _router(x1, mod_l, norm2_g, router_w, router_b):
    bsz, seq, d = x1.shape
    n_tok = bsz * seq
    n_exp = router_w.shape[1]
    n_win = n_tok // ROUTE_ROWS
    wins = ROUTE_STEP_WINDOWS
    rows = wins * ROUTE_ROWS
    tiles_per_seq = seq // rows
    stage = _stage_rows(n_exp)
    return pl.pallas_call(
        _router_kernel,
        grid=(n_win // wins,),
        in_specs=[
            pl.BlockSpec((rows, d), lambda i: (i, 0)),
            pl.BlockSpec((None, N_MOD, d), lambda i: (i // tiles_per_seq, 0, 0)),
            pl.BlockSpec((1, d), lambda i: (0, 0)),
            pl.BlockSpec((n_exp, d), lambda i: (0, 0)),
            pl.BlockSpec((n_exp, 1), lambda i: (0, 0)),
        ],
        out_specs=[
            pl.BlockSpec((wins, stage, d // 2 + LANES), lambda i: (i, 0, 0)),
            pl.BlockSpec((TOP_K, rows), lambda i: (0, i)),
            pl.BlockSpec((wins, n_exp, LANES), lambda i: (i, 0, 0)),
        ],
        out_shape=[
            jax.ShapeDtypeStruct((n_win, stage, d // 2 + LANES), U32),
            jax.ShapeDtypeStruct((TOP_K, n_tok), I32),
            jax.ShapeDtypeStruct((n_win, n_exp, LANES), I32),
        ],
        compiler_params=pltpu.CompilerParams(
            dimension_semantics=("arbitrary",), vmem_limit_bytes=VMEM_LIMIT_BYTES),
        name="router",
    )(x1.reshape(n_tok, d), mod_l, norm2_g.reshape(1, d), router_w.T, router_b.reshape(n_exp, 1))


def _expert_kernel(be_sm, first_sm, nact_sm, valid_sm, plo_sm, phi_sm, psrc_sm, pdst_sm, pn_sm, csrc_sm,
                   cn_sm, used_sm, xs_ref, w1_ref, b1_ref, w2_ref, b2_ref, ys_ref,
                   w1b_ref, w2b_ref, xbuf, ybuf, zbuf, in_sem, out_sem, tail_sem, *, n_win, stage):
    g = pl.program_id(0)
    n_steps = pl.num_programs(0)
    nact = nact_sm[0]
    d_ff = w2_ref.shape[0]
    half = ybuf.shape[2]

    def pieces(blk, fn, enabled=True, inline=0):
        lo, hi = plo_sm[blk], phi_sm[blk]

        def group(first):
            idx = [first + j for j in range(PIECE_UNROLL)]
            rows = [(p, pn_sm[p], psrc_sm[p], pdst_sm[p]) for p in idx]
            for p, n, src, dst in rows:
                @pl.when(enabled & (p < hi) & (n > 0))
                def _():
                    fn(pl.multiple_of(src, SEG_ALIGN), pl.multiple_of(dst, SEG_ALIGN),
                       pl.multiple_of(n, SEG_ALIGN))

        carried, carried_src = cn_sm[blk], csrc_sm[blk]

        @pl.when(enabled & (carried > 0))
        def _():
            fn(pl.multiple_of(carried_src, SEG_ALIGN), 0, pl.multiple_of(carried, SEG_ALIGN))

        for j in range(0, inline, PIECE_UNROLL):
            group(lo + j)

        def body(i, c):
            group(lo + inline + i * PIECE_UNROLL)
            return c
        left = jnp.maximum(hi - lo - inline, 0)
        trips = lax.shift_right_logical(left + (PIECE_UNROLL - 1), PIECE_UNROLL.bit_length() - 1)
        lax.fori_loop(0, jnp.where(enabled, trips, 0), body, 0)

    def in_copy(buf, src, dst, n):
        return pltpu.make_async_copy(
            xs_ref.at[pl.ds(src, n), :], xbuf.at[buf, pl.ds(dst, n), :], in_sem.at[buf])

    def out_copy(buf, src, dst, n):
        return pltpu.make_async_copy(
            ybuf.at[buf, pl.ds(dst, n), :], ys_ref.at[pl.ds(src, n), :], out_sem.at[buf])

    def valid_rows(blk):
        return pl.multiple_of(valid_sm[blk], SEG_ALIGN)

    def tail_copy(w):
        used = pl.multiple_of(used_sm[w], SEG_ALIGN)
        n = pl.multiple_of(stage - used, SEG_ALIGN)
        row = pl.multiple_of(w * stage + used, SEG_ALIGN)
        return pltpu.make_async_copy(zbuf.at[pl.ds(0, n), :], ys_ref.at[pl.ds(row, n), :], tail_sem)

    def for_windows(fn):
        def body(w, c):
            fn(w)
            return c
        lax.fori_loop(0, n_win, body, 0)

    @pl.when(g == 0)
    def _():
        xbuf[...] = jnp.zeros_like(xbuf)
        zbuf[...] = jnp.zeros_like(zbuf)
        for_windows(lambda w: tail_copy(w).start())
        pieces(0, lambda *a: in_copy(0, *a).start())

    @pl.when((g >= 2) & (g - 2 < nact))
    def _():
        out_copy(g % 2, 0, 0, valid_rows(g - 2)).wait()

    @pl.when(first_sm[g] == 1)
    def _():
        w1b_ref[...] = w1_ref[...].astype(BF16)
        w2b_ref[...] = w2_ref[...].astype(BF16)

    @pl.when(g < nact)
    def _():
        buf = g % 2
        pieces(g + 1, lambda *a: in_copy(1 - buf, *a).start(), enabled=g + 1 < nact, inline=INLINE_PIECES)
        in_copy(buf, 0, 0, valid_rows(g)).wait()
        packed = xbuf[buf]
        keep = lax.broadcasted_iota(I32, (EXPERT_ROWS, 1), 0) < valid_sm[g]
        xa, xb = _unpack_bf16_pairs(packed[:, :half])
        x = jnp.where(keep, jnp.concatenate([xa, xb], axis=-1), 0.0).astype(BF16)
        side = lax.bitcast_convert_type(packed[:, half:], F32)
        expert = be_sm[g].astype(F32)
        wcol = jnp.zeros((EXPERT_ROWS, 1), F32)
        for k in range(TOP_K):
            w_k = side[:, k:k + 1] + side[:, TOP_K + k:TOP_K + k + 1]
            wcol = jnp.where(side[:, 2 * TOP_K + k:2 * TOP_K + k + 1] == expert, w_k, wcol)
        wcol = jnp.where(keep, wcol, 0.0)
        hgu = jnp.dot(x, w1b_ref[...], preferred_element_type=F32) + b1_ref[...]
        gt = jnp.minimum(hgu[:, :d_ff], SWIGLU_LIMIT)
        up = jnp.clip(hgu[:, d_ff:], -SWIGLU_LIMIT, SWIGLU_LIMIT)
        act = (up + 1.0) * (gt * jax.nn.sigmoid(SWIGLU_ALPHA * gt))
        y = (jnp.dot(act.astype(BF16), w2b_ref[...], preferred_element_type=F32) + b2_ref[...]) * wcol
        y = y.astype(BF16).astype(F32)
        ybuf[buf] = _pack_bf16_pairs(y[:, :half], y[:, half:])
        pieces(g, lambda *a: out_copy(buf, *a).start(), inline=INLINE_PIECES)

    @pl.when(g == n_steps - 1)
    def _():
        for back in (1, 0):
            blk = g - back

            @pl.when((blk >= 0) & (blk < nact))
            def _():
                out_copy(blk % 2, 0, 0, valid_rows(blk)).wait()
        for_windows(lambda w: tail_copy(w).wait())


def _experts(xs, tables, layer, w1, b1, w2, b2):
    n_win, stage, xw = xs.shape
    depth, n_exp, d, d_hid = w1.shape
    d_ff = w2.shape[2]
    half = d // 2
    n_blocks = tables[0].shape[0]
    wmap = lambda g, be, *_: (layer, be[g], 0, 0)
    grid_spec = pltpu.PrefetchScalarGridSpec(
        num_scalar_prefetch=len(tables),
        grid=(n_blocks,),
        in_specs=[
            pl.BlockSpec(memory_space=pl.ANY),
            pl.BlockSpec((None, None, d, d_hid), wmap),
            pl.BlockSpec((None, None, 1, d_hid), wmap),
            pl.BlockSpec((None, None, d_ff, d), wmap),
            pl.BlockSpec((None, None, 1, d), wmap),
        ],
        out_specs=pl.BlockSpec(memory_space=pl.ANY),
        scratch_shapes=[
            pltpu.VMEM((d, d_hid), BF16),
            pltpu.VMEM((d_ff, d), BF16),
            pltpu.VMEM((2, EXPERT_ROWS, xw), U32),
            pltpu.VMEM((2, EXPERT_ROWS, half), U32),
            pltpu.VMEM((stage - TOP_K * ROUTE_ROWS, half), U32),
            pltpu.SemaphoreType.DMA((2,)),
            pltpu.SemaphoreType.DMA((2,)),
            pltpu.SemaphoreType.DMA,
        ],
    )
    ys = pl.pallas_call(
        functools.partial(_expert_kernel, n_win=n_win, stage=stage),
        grid_spec=grid_spec,
        out_shape=jax.ShapeDtypeStruct((n_win * stage, half), U32),
        compiler_params=pltpu.CompilerParams(
            dimension_semantics=("arbitrary",), vmem_limit_bytes=VMEM_LIMIT_BYTES),
        name="experts",
    )(*tables, xs.reshape(n_win * stage, xw), w1, b1.reshape(depth, n_exp, 1, d_hid), w2,
      b2.reshape(depth, n_exp, 1, d))
    return ys.reshape(n_win, stage, half)


def _combine_kernel(x_ref, mod_ref, pos_ref, fng_ref, ys_ref, o_ref, *, final_norm):
    wins, stage = ys_ref.shape[0], ys_ref.shape[1]
    rows = ROUTE_ROWS
    for wi in range(wins):
        sl = slice(wi * rows, (wi + 1) * rows)
        pos_rows = jnp.concatenate(
            [pos_ref[:, sl].astype(F32), jnp.full((LANES - TOP_K, rows), -1.0, F32)], axis=0)
        pos_t = jnp.transpose(pos_rows)
        c_iota = lax.broadcasted_iota(I32, (rows, stage), 1).astype(F32)
        hit = c_iota == pos_t[:, 0:1]
        for k in range(1, TOP_K):
            hit = hit | (c_iota == pos_t[:, k:k + 1])
        unperm = jnp.where(hit, 1.0, 0.0).astype(BF16)
        ya, yb = _unpack_bf16_pairs(ys_ref[wi])
        moe = jnp.concatenate(
            [jnp.dot(unperm, ya.astype(BF16), preferred_element_type=F32),
             jnp.dot(unperm, yb.astype(BF16), preferred_element_type=F32)], axis=-1)
        out = x_ref[sl, :] + mod_ref[5:6, :] * moe
        if final_norm:
            out = _rms_normalize(out) * fng_ref[...]
        o_ref[sl, :] = out


def _combine(x1, mod_l, pos, ys, final_norm_g, final_norm):
    bsz, seq, d = x1.shape
    n_tok = bsz * seq
    wins = COMBINE_STEP_WINDOWS
    rows = wins * ROUTE_ROWS
    tiles_per_seq = seq // rows
    n_win, stage, half = ys.shape
    out = pl.pallas_call(
        functools.partial(_combine_kernel, final_norm=final_norm),
        grid=(n_win // wins,),
        in_specs=[
            pl.BlockSpec((rows, d), lambda i: (i, 0)),
            pl.BlockSpec((None, N_MOD, d), lambda i: (i // tiles_per_seq, 0, 0)),
            pl.BlockSpec((TOP_K, rows), lambda i: (0, i)),
            pl.BlockSpec((1, d), lambda i: (0, 0)),
            pl.BlockSpec((wins, stage, half), lambda i: (i, 0, 0)),
        ],
        out_specs=pl.BlockSpec((rows, d), lambda i: (i, 0)),
        out_shape=jax.ShapeDtypeStruct((n_tok, d), F32),
        compiler_params=pltpu.CompilerParams(
            dimension_semantics=("arbitrary",), vmem_limit_bytes=VMEM_LIMIT_BYTES),
        name="combine",
    )(x1.reshape(n_tok, d), mod_l, pos, final_norm_g.reshape(1, d), ys)
    return out.reshape(bsz, seq, d)


def _segment_tables(cnt, stage):
    n_win, n_exp = cnt.shape
    seg_len = _round_up(cnt, SEG_ALIGN)
    seg_src = jnp.cumsum(seg_len, axis=1) - seg_len
    used = jnp.sum(seg_len, axis=1)
    seg_off = jnp.cumsum(seg_len, axis=0) - seg_len
    reg_len = jnp.sum(seg_len, axis=0)
    reg_pad = _round_up(reg_len, EXPERT_ROWS)
    pad_end = jnp.cumsum(reg_pad)
    base = pad_end - reg_pad
    max_rows = n_win * (TOP_K * ROUTE_ROWS + n_exp * (SEG_ALIGN - 1)) + n_exp * (EXPERT_ROWS - SEG_ALIGN)
    n_blocks = -(-max_rows // EXPERT_ROWS)
    n_active = pad_end[-1] // EXPERT_ROWS
    gidx = jnp.arange(n_blocks, dtype=I32)
    blk_row = (gidx * EXPERT_ROWS)[:, None]
    in_region = (base[None, :] <= blk_row) & (blk_row < pad_end[None, :])
    valid = jnp.sum(jnp.where(in_region, jnp.clip((base + reg_len)[None, :] - blk_row, 0, EXPERT_ROWS), 0), axis=1)
    first = jnp.any(in_region & (base[None, :] == blk_row), axis=1)
    last_row = (jnp.minimum(gidx, n_active - 1) * EXPERT_ROWS)[:, None]
    block_e = jnp.minimum(jnp.sum(pad_end[None, :] <= last_row, axis=1), n_exp - 1)

    glob = (base[None, :] + seg_off).T.reshape(-1)
    seg_n = seg_len.T.reshape(-1)
    src = (jnp.arange(n_win, dtype=I32)[:, None] * stage + seg_src).T.reshape(-1)
    head_blk = glob // EXPERT_ROWS
    head_dst = glob % EXPERT_ROWS
    head_n = jnp.minimum(seg_n, EXPERT_ROWS - head_dst)
    carry_n = seg_n - head_n
    p_lo = jnp.sum(head_blk[None, :] < gidx[:, None], axis=1)
    p_hi = jnp.sum(head_blk[None, :] <= gidx[:, None], axis=1)
    carried = (head_blk[None, :] + 1 == gidx[:, None]) & (carry_n[None, :] > 0)
    c_n = jnp.sum(jnp.where(carried, carry_n[None, :], 0), axis=1)
    c_src = jnp.sum(jnp.where(carried, (src + head_n)[None, :], 0), axis=1)
    p_src, p_dst, p_n = (jnp.pad(t, (0, INLINE_PIECES + PIECE_UNROLL)) for t in (src, head_dst, head_n))
    tables = (block_e, first, n_active.reshape(1), valid, p_lo, p_hi, p_src, p_dst, p_n, c_src, c_n, used)
    return tuple(t.astype(I32) for t in tables)


def _moe_layer(x1, mod_l, norm2_g, router_w, router_b, layer, w1, b1, w2, b2, final_norm_g, final_norm):
    xs, pos, cnt = _router(x1, mod_l, norm2_g, router_w, router_b)
    tables = _segment_tables(cnt[:, :, 0], xs.shape[1])
    ys = _experts(xs, tables, layer, w1, b1, w2, b2)
    return _combine(x1, mod_l, pos, ys, final_norm_g, final_norm)


def kernel(x, c, norm1_g, norm2_g, w_ada, b_ada, w_in, conv_a_w, sgu_w, sgu_b, sgu_ln_g, sgu_ln_b,
           conv_c_w, conv_c_b, rglru_gate_w, rglru_gate_b, rglru_lambda, mix_norm_g, w_out, router_w,
           router_b, expert_w1, expert_b1, expert_w2, expert_b2, final_norm_g):
    depth = w_ada.shape[0]
    mod = _adaln_modulation(c, w_ada, b_ada)
    for l in range(depth):
        x1 = _mixer(x, mod[l], norm1_g[l], w_in[l], conv_a_w[l], sgu_w[l], sgu_b[l], sgu_ln_g[l],
                    sgu_ln_b[l], conv_c_w[l], conv_c_b[l], rglru_gate_w[l], rglru_gate_b[l],
                    rglru_lambda[l], mix_norm_g[l], w_out[l])
        x = _moe_layer(x1, mod[l], norm2_g[l], router_w[l], router_b[l], l, expert_w1, expert_b1,
                       expert_w2, expert_b2, final_norm_g, l == depth - 1)
    return x
```
